```python
import math
import jax, jax.numpy as jnp
from jax import lax
import numpy as np

D_MODEL = 1024
BATCH = 8
SEQ = 4096
DEPTH = 4

CHUNK = 64
D_MIX = D_MODEL
D_POOL = D_MIX // 2
POOL_WINDOWS = (2, 4, 8, 16)
N_POOL_GROUPS = len(POOL_WINDOWS)
POOL_GROUP = D_POOL // N_POOL_GROUPS
D_MLSTM = D_MIX - D_POOL
N_MLSTM_HEADS = 4
MLSTM_HEAD_DIM = D_MLSTM // N_MLSTM_HEADS
CONV_WIDTH = 4
D_IN = D_POOL + 4 * D_MLSTM + 2 * N_MLSTM_HEADS
D_FF_DENSE = ((8 * D_MODEL // 3 + 127) // 128) * 128
N_EXPERTS = 8
TOP_K = 2
D_FF_EXPERT = 7 * D_MODEL // 2
N_DENSE_LAYERS = (DEPTH + 1) // 2
N_MOE_LAYERS = DEPTH // 2
ALPHA = (2 * DEPTH) ** 0.25
BETA = (8 * DEPTH) ** -0.25
LN_EPS = 1e-5

kernel_name = "hybrid_pool_mlstm_moe_deepnorm"


def layer_norm(x, g, b):
    xf = x.astype(jnp.float32)
    mu = jnp.mean(xf, axis=-1, keepdims=True)
    xc = xf - mu
    var = jnp.mean(xc * xc, axis=-1, keepdims=True)
    return (xc * lax.rsqrt(var + LN_EPS) * g.astype(jnp.float32) + b.astype(jnp.float32)).astype(x.dtype)


def pool_mixer(u, w_maps, scale):
    bsz, s, _ = u.shape
    uf = u.astype(jnp.float32)
    P = jnp.concatenate([jnp.zeros_like(uf[:, :1]), jnp.cumsum(uf, axis=1)], axis=1)
    t = jnp.arange(s)
    outs = []
    for g, w in enumerate(POOL_WINDOWS):
        sl = slice(g * POOL_GROUP, (g + 1) * POOL_GROUP)
        Pg = P[:, :, sl]
        lead = jnp.pad(Pg[:, : s + 1 - w], ((0, 0), (w - 1, 0), (0, 0)))
        count = jnp.minimum(t + 1, w).astype(jnp.float32)[None, :, None]
        outs.append((Pg[:, 1:] - lead) / count - uf[:, :, sl])
    d = jnp.stack(outs, axis=2)
    y = jnp.einsum('bsgc,gcd->bsgd', d, w_maps.astype(jnp.float32)).reshape(bsz, s, D_POOL)
    return (y * scale.astype(jnp.float32)).astype(u.dtype)


def causal_conv_silu(u, w, b):
    K = w.shape[0]
    s = u.shape[1]
    up = jnp.pad(u, ((0, 0), (K - 1, 0), (0, 0)))
    y = up[:, 0:s] * w[0]
    for j in range(1, K):
        y = y + up[:, j:j + s] * w[j]
    return jax.nn.silu(y + b)


def mlstm_chunkwise(q, k, v, i_pre, f_pre):
    bsz, s, H, Dh = q.shape
    nc, L = s // CHUNK, CHUNK

    def chunks(a):
        return a.astype(jnp.float32).reshape(bsz, nc, L, H, Dh).transpose(0, 3, 1, 2, 4)

    def chunks_g(a):
        return a.astype(jnp.float32).reshape(bsz, nc, L, H).transpose(0, 3, 1, 2)

    qc = chunks(q)
    kc = chunks(k) * (Dh ** -0.5)
    vc = chunks(v)
    ig = chunks_g(i_pre)
    logf = jax.nn.log_sigmoid(chunks_g(f_pre))
    b = jnp.cumsum(logf, axis=-1)
    g = b[..., -1]

    a = g[..., None] - b + ig
    m_loc = jnp.max(a, axis=-1)
    wa = jnp.exp(a - m_loc[..., None])
    C_loc = jnp.einsum('bhclv,bhclk->bhcvk', wa[..., None] * vc, kc)
    n_loc = jnp.einsum('bhcl,bhclk->bhck', wa, kc)

    def step(carry, inp):
        C, n, m = carry
        g_c, m_l, C_l, n_l = inp
        m_new = jnp.maximum(g_c + m, m_l)
        s_old = jnp.exp(g_c + m - m_new)
        s_loc = jnp.exp(m_l - m_new)
        C_new = s_old[..., None, None] * C + s_loc[..., None, None] * C_l
        n_new = s_old[..., None] * n + s_loc[..., None] * n_l
        return (C_new, n_new, m_new), (C, n, m)

    init = (jnp.zeros((bsz, H, Dh, Dh), jnp.float32), jnp.zeros((bsz, H, Dh), jnp.float32),
            jnp.zeros((bsz, H), jnp.float32))
    xs = (jnp.moveaxis(g, 2, 0), jnp.moveaxis(m_loc, 2, 0), jnp.moveaxis(C_loc, 2, 0), jnp.moveaxis(n_loc, 2, 0))
    _, (C_prev, n_prev, m_prev) = lax.scan(step, init, xs)
    C_prev = jnp.moveaxis(C_prev, 0, 2)
    n_prev = jnp.moveaxis(n_prev, 0, 2)
    m_prev = jnp.moveaxis(m_prev, 0, 2)

    Dlog = b[..., :, None] - b[..., None, :] + ig[..., None, :]
    causal = jnp.tril(jnp.ones((L, L), dtype=bool))
    Dlog = jnp.where(causal, Dlog, -jnp.inf)
    inter = b + m_prev[..., None]
    m_t = jnp.maximum(jnp.max(Dlog, axis=-1), inter)
    W = jnp.exp(Dlog - m_t[..., None]) * jnp.einsum('bhctk,bhcsk->bhcts', qc, kc)
    s_inter = jnp.exp(inter - m_t)
    num = jnp.einsum('bhcts,bhcsv->bhctv', W, vc) + s_inter[..., None] * jnp.einsum('bhcvk,bhctk->bhctv', C_prev, qc)
    den = jnp.sum(W, axis=-1) + s_inter * jnp.einsum('bhck,bhctk->bhct', n_prev, qc)
    h = num / jnp.maximum(jnp.abs(den), jnp.exp(-m_t))[..., None]
    return h.transpose(0, 2, 3, 1, 4).reshape(bsz, s, H, Dh)


def hybrid_mixer(x, w_in, b_gates, conv_w, conv_b, pool_w, pool_scale, mlstm_g, w_out):
    bsz, s, _ = x.shape
    H, Dh = N_MLSTM_HEADS, MLSTM_HEAD_DIM
    z = x @ w_in
    o0 = D_POOL
    u_pool = z[..., :o0]
    qk = z[..., o0:o0 + 2 * D_MLSTM]
    v = z[..., o0 + 2 * D_MLSTM:o0 + 3 * D_MLSTM]
    o_pre = z[..., o0 + 3 * D_MLSTM:o0 + 4 * D_MLSTM]
    gates = z[..., o0 + 4 * D_MLSTM:] + b_gates

    y_pool = pool_mixer(u_pool, pool_w, pool_scale)

    qk = causal_conv_silu(qk, conv_w, conv_b)
    q = qk[..., :D_MLSTM].reshape(bsz, s, H, Dh)
    k = qk[..., D_MLSTM:].reshape(bsz, s, H, Dh)
    h = mlstm_chunkwise(q, k, v.reshape(bsz, s, H, Dh), gates[..., :H], gates[..., H:])
    mu = jnp.mean(h, axis=-1, keepdims=True)
    hc = h - mu
    h = hc * lax.rsqrt(jnp.mean(hc * hc, axis=-1, keepdims=True) + LN_EPS)
    h = h.reshape(bsz, s, D_MLSTM) * mlstm_g.astype(jnp.float32)
    y_m = (jax.nn.sigmoid(o_pre.astype(jnp.float32)) * h).astype(x.dtype)

    return jnp.concatenate([y_pool, y_m], axis=-1) @ w_out


def swiglu(x, wg, wu, wd):
    return (jax.nn.silu(x @ wg) * (x @ wu)) @ wd


def moe_swiglu(x, w_router, b_router, wg, wu, wd):
    logits = (x @ w_router).astype(jnp.float32) + b_router.astype(jnp.float32)
    top_v, top_i = lax.top_k(logits, TOP_K)
    top_w = jax.nn.softmax(top_v, axis=-1)
    combine = jnp.sum(jax.nn.one_hot(top_i, N_EXPERTS, dtype=jnp.float32) * top_w[..., None], axis=-2)
    combine = combine.astype(x.dtype)
    out = jnp.zeros_like(x)
    for e in range(N_EXPERTS):
        out = out + combine[..., e:e + 1] * swiglu(x, wg[e], wu[e], wd[e])
    return out


def setup_inputs(seed: int = 0) -> dict:
    key = jax.random.key(seed)
    ks = jax.random.split(key, 24)
    f32 = jnp.float32
    H = N_MLSTM_HEADS
    nrm = lambda k, shape, scale: jax.random.normal(k, shape, f32) * scale

    x = nrm(ks[0], (BATCH, SEQ, D_MODEL), 1.0)
    w_in = nrm(ks[1], (DEPTH, D_MODEL, D_IN), D_MODEL ** -0.5)
    gate_bias0 = jnp.concatenate([jnp.zeros((H,), f32), jnp.linspace(3.0, 6.0, H, dtype=f32)])
    b_gates = gate_bias0[None, :] + nrm(ks[2], (DEPTH, 2 * H), 0.1)
    conv_w = nrm(ks[3], (DEPTH, CONV_WIDTH, 2 * D_MLSTM), CONV_WIDTH ** -0.5)
    conv_b = nrm(ks[4], (DEPTH, 2 * D_MLSTM), 0.02)
    pool_w = nrm(ks[5], (DEPTH, N_POOL_GROUPS, POOL_GROUP, POOL_GROUP), POOL_GROUP ** -0.5)
    pool_scale = 1.0 + nrm(ks[6], (DEPTH, D_POOL), 0.1)
    mlstm_g = 1.0 + nrm(ks[7], (DEPTH, D_MLSTM), 0.1)
    w_out = nrm(ks[8], (DEPTH, D_MIX, D_MODEL), BETA * D_MIX ** -0.5)
    ln1_g = 1.0 + nrm(ks[9], (DEPTH, D_MODEL), 0.05)
    ln1_b = nrm(ks[10], (DEPTH, D_MODEL), 0.02)
    ln2_g = 1.0 + nrm(ks[11], (DEPTH, D_MODEL), 0.05)
    ln2_b = nrm(ks[12], (DEPTH, D_MODEL), 0.02)
    wg_dense = nrm(ks[13], (N_DENSE_LAYERS, D_MODEL, D_FF_DENSE), D_MODEL ** -0.5)
    wu_dense = nrm(ks[14], (N_DENSE_LAYERS, D_MODEL, D_FF_DENSE), D_MODEL ** -0.5)
    wd_dense = nrm(ks[15], (N_DENSE_LAYERS, D_FF_DENSE, D_MODEL), BETA * D_FF_DENSE ** -0.5)
    w_router = nrm(ks[16], (N_MOE_LAYERS, D_MODEL, N_EXPERTS), D_MODEL ** -0.5)
    b_router = nrm(ks[17], (N_MOE_LAYERS, N_EXPERTS), 0.01)
    wg_exp = nrm(ks[18], (N_MOE_LAYERS, N_EXPERTS, D_MODEL, D_FF_EXPERT), D_MODEL ** -0.5)
    wu_exp = nrm(ks[19], (N_MOE_LAYERS, N_EXPERTS, D_MODEL, D_FF_EXPERT), D_MODEL ** -0.5)
    wd_exp = nrm(ks[20], (N_MOE_LAYERS, N_EXPERTS, D_FF_EXPERT, D_MODEL), BETA * D_FF_EXPERT ** -0.5)
    return {"x": x, "w_in": w_in, "b_gates": b_gates, "conv_w": conv_w, "conv_b": conv_b,
            "pool_w": pool_w, "pool_scale": pool_scale, "mlstm_g": mlstm_g, "w_out": w_out,
            "ln1_g": ln1_g, "ln1_b": ln1_b, "ln2_g": ln2_g, "ln2_b": ln2_b,
            "wg_dense": wg_dense, "wu_dense": wu_dense, "wd_dense": wd_dense,
            "w_router": w_router, "b_router": b_router,
            "wg_exp": wg_exp, "wu_exp": wu_exp, "wd_exp": wd_exp}


def reference(x, w_in, b_gates, conv_w, conv_b, pool_w, pool_scale, mlstm_g, w_out,
              ln1_g, ln1_b, ln2_g, ln2_b, wg_dense, wu_dense, wd_dense,
              w_router, b_router, wg_exp, wu_exp, wd_exp):
    for l in range(DEPTH):
        mix = hybrid_mixer(x, w_in[l], b_gates[l], conv_w[l], conv_b[l], pool_w[l],
                           pool_scale[l], mlstm_g[l], w_out[l])
        h = layer_norm(ALPHA * x + mix, ln1_g[l], ln1_b[l])
        j = l // 2
        if l % 2 == 0:
            f = swiglu(h, wg_dense[j], wu_dense[j], wd_dense[j])
        else:
            f = moe_swiglu(h, w_router[j], b_router[j], wg_exp[j], wu_exp[j], wd_exp[j])
        x = layer_norm(ALPHA * h + f, ln2_g[l], ln2_b[l])
    return x
```

```python
import functools

import jax
import jax.numpy as jnp
from jax import lax
from jax.experimental import pallas as pl
from jax.experimental.pallas import tpu as pltpu

F32 = jnp.float32
BF16 = jnp.bfloat16

D_MODEL = 1024
DEPTH = 4
D_POOL = 512
POOL_WINDOWS = (2, 4, 8, 16)
POOL_GROUP = 128
MAX_WINDOW = max(POOL_WINDOWS)
D_MLSTM = 512
N_HEADS = 4
HEAD_DIM = 128
CONV_WIDTH = 4
N_GATES = 2 * N_HEADS
D_IN = D_POOL + 4 * D_MLSTM + N_GATES
N_EXPERTS = 8
TOP_K = 2
ALPHA = (2 * DEPTH) ** 0.25
LN_EPS = 1e-5

LANES = 128
SUBLANES = 8
D_IN_PAD = D_POOL + 4 * D_MLSTM + LANES
VMEM_LIMIT_BYTES = 56 * 1024 * 1024

OFF_QK = D_POOL
OFF_V = OFF_QK + 2 * D_MLSTM
OFF_O = OFF_V + D_MLSTM
OFF_G = OFF_O + D_MLSTM

MIX_TS = 512
MIX_L = 256
CONV_PAD = SUBLANES
POOL_PAD = 16


def _dot(a, b):
    return jnp.dot(a, b, preferred_element_type=F32)


def _sigmoid(x):
    return 1.0 / (1.0 + jnp.exp(-x))


def _layer_norm_rows(r, g, b):
    mu = jnp.mean(r, axis=-1, keepdims=True)
    rc = r - mu
    var = jnp.mean(rc * rc, axis=-1, keepdims=True)
    return rc * lax.rsqrt(var + LN_EPS) * g + b


def _mixer_kernel(x_ref, win_ref, bg_ref, cw_ref, cb_ref, pw_ref, ps_ref, mg_ref, wout_ref,
                  lg_ref, lb_ref, h_ref,
                  pbuf, cbuf, q_s, k_s, ycat, ct_s, n_s, m_s):
    ts, L = MIX_TS, MIX_L
    s_idx = pl.program_id(1)

    @pl.when(s_idx == 0)
    def _():
        pbuf[0:POOL_PAD, :] = jnp.zeros((POOL_PAD, D_POOL), F32)
        cbuf[0:CONV_PAD, :] = jnp.zeros((CONV_PAD, 2 * D_MLSTM), F32)
        ct_s[...] = jnp.zeros_like(ct_s)
        n_s[...] = jnp.zeros_like(n_s)
        m_s[...] = jnp.zeros_like(m_s)

    x = x_ref[...]
    xb = x.astype(BF16)
    pbuf[POOL_PAD:POOL_PAD + ts, :] = _dot(xb, win_ref[:, 0:OFF_QK])
    cbuf[CONV_PAD:CONV_PAD + ts, :] = _dot(xb, win_ref[:, OFF_QK:OFF_V])
    v = _dot(xb, win_ref[:, OFF_V:OFF_O]).astype(BF16)
    o_pre = _dot(xb, win_ref[:, OFF_O:OFF_G])
    gates = _dot(xb, win_ref[:, OFF_G:D_IN_PAD]) + bg_ref[...]

    t_glob = lax.broadcasted_iota(jnp.int32, (ts, 1), 0) + s_idx * ts
    for g, w in enumerate(POOL_WINDOWS):
        cs = slice(g * POOL_GROUP, (g + 1) * POOL_GROUP)
        u = pbuf[POOL_PAD:POOL_PAD + ts, cs]
        acc = u
        for j in range(1, w):
            acc = acc + pbuf[POOL_PAD - j:POOL_PAD - j + ts, cs]
        cnt = jnp.minimum(t_glob + 1, w).astype(F32)
        d = acc / cnt - u
        y = _dot(d.astype(BF16), pw_ref[g]) * ps_ref[:, cs]
        ycat[:, cs] = y.astype(BF16)
    pbuf[0:POOL_PAD, :] = pbuf[ts:ts + POOL_PAD, :]

    for c in range(2 * D_MLSTM // LANES):
        cs = slice(c * LANES, (c + 1) * LANES)
        y = cb_ref[:, cs] + cbuf[CONV_PAD:CONV_PAD + ts, cs] * cw_ref[CONV_WIDTH - 1:CONV_WIDTH, cs]
        for j in range(1, CONV_WIDTH):
            y = y + cbuf[CONV_PAD - j:CONV_PAD - j + ts, cs] * cw_ref[CONV_WIDTH - 1 - j:CONV_WIDTH - j, cs]
        y = y * _sigmoid(y)
        if c < N_HEADS:
            q_s[:, cs] = y.astype(BF16)
        else:
            ks = slice((c - N_HEADS) * LANES, (c - N_HEADS + 1) * LANES)
            k_s[:, ks] = (y * (HEAD_DIM ** -0.5)).astype(BF16)
    cbuf[0:CONV_PAD, :] = cbuf[ts:ts + CONV_PAD, :]

    g_row = gates.T[0:N_GATES, :]
    logf_row = jnp.minimum(g_row, 0.0) - jnp.log1p(jnp.exp(-jnp.abs(g_row)))
    lane_in_chunk = lax.broadcasted_iota(jnp.int32, (N_GATES, ts), 1) % L
    b_row = logf_row
    sh = 1
    while sh < L:
        b_row = b_row + jnp.where(lane_in_chunk >= sh, pltpu.roll(b_row, sh, axis=1), 0.0)
        sh *= 2
    b_col = jnp.concatenate([b_row, jnp.zeros((LANES - N_GATES, ts), F32)], axis=0).T

    tri = (lax.broadcasted_iota(jnp.int32, (L, L), 1) <= lax.broadcasted_iota(jnp.int32, (L, L), 0))
    dn_nt = (((1,), (1,)), ((), ()))
    dn_tn = (((0,), (0,)), ((), ()))

    for c in range(ts // L):
        rs = slice(c * L, (c + 1) * L)
        for h in range(N_HEADS):
            hs = slice(h * HEAD_DIM, (h + 1) * HEAD_DIM)
            bc = b_col[rs, N_HEADS + h:N_HEADS + h + 1]
            ic = gates[rs, h:h + 1]
            br = b_row[N_HEADS + h:N_HEADS + h + 1, rs]
            ir = g_row[h:h + 1, rs]
            m_prev = m_s[h:h + 1, 0:1]
            q_h = q_s[rs, hs]
            k_h = k_s[rs, hs]
            v_h = v[rs, hs]

            dlog = jnp.where(tri, bc + (ir - br), -jnp.inf)
            inter = bc + m_prev
            m_t = jnp.maximum(jnp.max(dlog, axis=1, keepdims=True), inter)
            s_qk = lax.dot_general(q_h, k_h, dn_nt, preferred_element_type=F32)
            wmat = jnp.exp(dlog - m_t) * s_qk
            s_inter = jnp.exp(inter - m_t)
            num = _dot(wmat.astype(BF16), v_h) + s_inter * _dot(q_h, ct_s[h].astype(BF16))
            qn = jnp.sum(q_h.astype(F32) * n_s[h:h + 1, :], axis=1, keepdims=True)
            den = jnp.sum(wmat, axis=1, keepdims=True) + s_inter * qn
            hh = num / jnp.maximum(jnp.abs(den), jnp.exp(-m_t))

            mu = jnp.mean(hh, axis=1, keepdims=True)
            hc = hh - mu
            var = jnp.mean(hc * hc, axis=1, keepdims=True)
            hn = hc * lax.rsqrt(var + LN_EPS) * mg_ref[:, hs]
            ym = _sigmoid(o_pre[rs, hs]) * hn
            ycat[rs, D_POOL + h * HEAD_DIM:D_POOL + (h + 1) * HEAD_DIM] = ym.astype(BF16)

            g_tot = bc[L - 1:L, :]
            a_loc = g_tot + ic - bc
            m_loc = jnp.max(a_loc, axis=0, keepdims=True)
            wa = jnp.exp(a_loc - m_loc)
            wv = (wa * v_h.astype(F32)).astype(BF16)
            ct_loc = lax.dot_general(k_h, wv, dn_tn, preferred_element_type=F32)
            n_loc = jnp.sum(wa * k_h.astype(F32), axis=0, keepdims=True)
            m_new = jnp.maximum(g_tot + m_prev, m_loc)
            s_old = jnp.exp(g_tot + m_prev - m_new)
            s_loc = jnp.exp(m_loc - m_new)
            ct_s[h] = s_old * ct_s[h] + s_loc * ct_loc
            n_s[h:h + 1, :] = s_old * n_s[h:h + 1, :] + s_loc * n_loc
            m_s[h:h + 1, :] = jnp.broadcast_to(m_new, (1, LANES))

    mix = _dot(ycat[...], wout_ref[...])
    h_ref[...] = _layer_norm_rows(ALPHA * x + mix, lg_ref[...], lb_ref[...])


def _mixer_layer(x2d, bsz, seq, w_in, b_gates, conv_w, conv_b, pool_w, pool_scale, mlstm_g, w_out,
                 ln_g, ln_b):
    ts = MIX_TS
    assert seq % ts == 0 and ts % MIX_L == 0
    n_s_tiles = seq // ts
    w_in_p = jnp.pad(w_in, ((0, 0), (0, D_IN_PAD - D_IN))).astype(BF16)
    bg_p = jnp.pad(b_gates, (0, LANES - N_GATES)).reshape(1, LANES)
    row = lambda a: a.reshape(1, -1)
    const2 = lambda b, s: (0, 0)
    const3 = lambda b, s: (0, 0, 0)
    return pl.pallas_call(
        _mixer_kernel,
        grid=(bsz, n_s_tiles),
        in_specs=[
            pl.BlockSpec((ts, D_MODEL), lambda b, s: (b * n_s_tiles + s, 0)),
            pl.BlockSpec((D_MODEL, D_IN_PAD), const2),
            pl.BlockSpec((1, LANES), const2),
            pl.BlockSpec((CONV_WIDTH, 2 * D_MLSTM), const2),
            pl.BlockSpec((1, 2 * D_MLSTM), const2),
            pl.BlockSpec((len(POOL_WINDOWS), POOL_GROUP, POOL_GROUP), const3),
            pl.BlockSpec((1, D_POOL), const2),
            pl.BlockSpec((1, D_MLSTM), const2),
            pl.BlockSpec((D_MODEL, D_MODEL), const2),
            pl.BlockSpec((1, D_MODEL), const2),
            pl.BlockSpec((1, D_MODEL), const2),
        ],
        out_specs=pl.BlockSpec((ts, D_MODEL), lambda b, s: (b * n_s_tiles + s, 0)),
        out_shape=jax.ShapeDtypeStruct((bsz * seq, D_MODEL), F32),
        scratch_shapes=[
            pltpu.VMEM((POOL_PAD + ts, D_POOL), F32),
            pltpu.VMEM((CONV_PAD + ts, 2 * D_MLSTM), F32),
            pltpu.VMEM((ts, D_MLSTM), BF16),
            pltpu.VMEM((ts, D_MLSTM), BF16),
            pltpu.VMEM((ts, D_MODEL), BF16),
            pltpu.VMEM((N_HEADS, HEAD_DIM, HEAD_DIM), F32),
            pltpu.VMEM((SUBLANES, HEAD_DIM), F32),
            pltpu.VMEM((SUBLANES, LANES), F32),
        ],
        compiler_params=pltpu.CompilerParams(
            dimension_semantics=("arbitrary", "arbitrary"),
            vmem_limit_bytes=VMEM_LIMIT_BYTES),
        name="mixer",
    )(x2d, w_in_p, bg_p, conv_w, row(conv_b), pool_w.astype(BF16), row(pool_scale), row(mlstm_g),
      w_out.astype(BF16), row(ln_g), row(ln_b))


DENSE_TM = 512
DENSE_TF = 1408


def _swiglu_partial(xb, wg, wu, wd):
    g = _dot(xb, wg)
    u = _dot(xb, wu)
    a = (g * _sigmoid(g) * u).astype(BF16)
    return _dot(a, wd)


def _ffn_dense_kernel(x_ref, wg_ref, wu_ref, wd_ref, lg_ref, lb_ref, o_ref, xb_s, acc_s):
    f = pl.program_id(1)

    @pl.when(f == 0)
    def _():
        xb_s[...] = x_ref[...].astype(BF16)
        acc_s[...] = jnp.zeros_like(acc_s)

    acc_s[...] += _swiglu_partial(xb_s[...], wg_ref[...], wu_ref[...], wd_ref[...])

    @pl.when(f == pl.num_programs(1) - 1)
    def _():
        o_ref[...] = _layer_norm_rows(ALPHA * x_ref[...] + acc_s[...], lg_ref[...], lb_ref[...])


def _ffn_dense_layer(h2d, wg, wu, wd, ln_g, ln_b):
    n_tok = h2d.shape[0]
    d_ff = wg.shape[1]
    tm, tf = DENSE_TM, DENSE_TF
    assert n_tok % tm == 0 and d_ff % tf == 0
    row = lambda a: a.reshape(1, -1)
    return pl.pallas_call(
        _ffn_dense_kernel,
        grid=(n_tok // tm, d_ff // tf),
        in_specs=[
            pl.BlockSpec((tm, D_MODEL), lambda i, f: (i, 0)),
            pl.BlockSpec((D_MODEL, tf), lambda i, f: (0, f)),
            pl.BlockSpec((D_MODEL, tf), lambda i, f: (0, f)),
            pl.BlockSpec((tf, D_MODEL), lambda i, f: (f, 0)),
            pl.BlockSpec((1, D_MODEL), lambda i, f: (0, 0)),
            pl.BlockSpec((1, D_MODEL), lambda i, f: (0, 0)),
        ],
        out_specs=pl.BlockSpec((tm, D_MODEL), lambda i, f: (i, 0)),
        out_shape=jax.ShapeDtypeStruct((n_tok, D_MODEL), F32),
        scratch_shapes=[pltpu.VMEM((tm, D_MODEL), BF16), pltpu.VMEM((tm, D_MODEL), F32)],
        compiler_params=pltpu.CompilerParams(
            dimension_semantics=("arbitrary", "arbitrary"),
            vmem_limit_bytes=VMEM_LIMIT_BYTES),
        name="ffn_dense",
    )(h2d, wg.astype(BF16), wu.astype(BF16), wd.astype(BF16), row(ln_g), row(ln_b))


ROUTE_TS = 512
DISPATCH_TS = 512
MOE_TM = 1024
MOE_TF = 512
ROUTE_COLS = 8


def _route_kernel(h_ref, wr_ref, br_ref, idx_ref, wts_ref, cnt_ref, carry_s):
    ts = ROUTE_TS
    i = pl.program_id(0)

    @pl.when(i == 0)
    def _():
        carry_s[...] = jnp.zeros_like(carry_s)

    logits = jnp.dot(h_ref[...], wr_ref[...], precision=lax.Precision.HIGHEST,
                     preferred_element_type=F32) + br_ref[...]
    lane = lax.broadcasted_iota(jnp.int32, (ts, LANES), 1)
    lg = jnp.where(lane < N_EXPERTS, logits, -jnp.inf)
    m1 = jnp.max(lg, axis=1, keepdims=True)
    i1 = jnp.min(jnp.where(lg == m1, lane, LANES), axis=1, keepdims=True)
    oh1 = lane == i1
    lg2 = jnp.where(oh1, -jnp.inf, lg)
    m2 = jnp.max(lg2, axis=1, keepdims=True)
    i2 = jnp.min(jnp.where(lg2 == m2, lane, LANES), axis=1, keepdims=True)
    oh2 = lane == i2
    e2 = jnp.exp(m2 - m1)
    w1 = 1.0 / (1.0 + e2)
    w2 = e2 / (1.0 + e2)

    sel = jnp.where(oh1 | oh2, 1.0, 0.0)
    strict_lower = jnp.where(lax.broadcasted_iota(jnp.int32, (ts, ts), 1)
                             < lax.broadcasted_iota(jnp.int32, (ts, ts), 0), 1.0, 0.0).astype(BF16)
    before = _dot(strict_lower, sel.astype(BF16)) + carry_s[...]
    r1 = jnp.sum(jnp.where(oh1, before, 0.0), axis=1, keepdims=True).astype(jnp.int32)
    r2 = jnp.sum(jnp.where(oh2, before, 0.0), axis=1, keepdims=True).astype(jnp.int32)
    carry_s[...] += jnp.sum(sel, axis=0, keepdims=True)

    col = lax.broadcasted_iota(jnp.int32, (ts, ROUTE_COLS), 1)
    idx_ref[...] = jnp.where(col == 0, i1, jnp.where(col == 1, i2, jnp.where(col == 2, r1, r2)))
    wts_ref[...] = jnp.where(col == 0, w1, w2)
    cnt_ref[...] = carry_s[...]


def _route(h2d, w_router, b_router):
    n_tok = h2d.shape[0]
    ts = ROUTE_TS
    assert n_tok % ts == 0
    wr = jnp.pad(w_router, ((0, 0), (0, LANES - N_EXPERTS)))
    br = jnp.pad(b_router, (0, LANES - N_EXPERTS)).reshape(1, LANES)
    return pl.pallas_call(
        _route_kernel,
        grid=(n_tok // ts,),
        in_specs=[
            pl.BlockSpec((ts, D_MODEL), lambda i: (i, 0)),
            pl.BlockSpec((D_MODEL, LANES), lambda i: (0, 0)),
            pl.BlockSpec((1, LANES), lambda i: (0, 0)),
        ],
        out_specs=[
            pl.BlockSpec((ts, ROUTE_COLS), lambda i: (i, 0)),
            pl.BlockSpec((ts, ROUTE_COLS), lambda i: (i, 0)),
            pl.BlockSpec((1, LANES), lambda i: (0, 0)),
        ],
        out_shape=[
            jax.ShapeDtypeStruct((n_tok, ROUTE_COLS), jnp.int32),
            jax.ShapeDtypeStruct((n_tok, ROUTE_COLS), F32),
            jax.ShapeDtypeStruct((1, LANES), F32),
        ],
        scratch_shapes=[pltpu.VMEM((1, LANES), F32)],
        compiler_params=pltpu.CompilerParams(
            dimension_semantics=("arbitrary",), vmem_limit_bytes=VMEM_LIMIT_BYTES),
        name="moe_route",
    )(h2d, wr, br)


def _row_copy(src_ref, src_row, dst_ref, dst_row, sem):
    return pltpu.make_async_copy(src_ref.at[pl.ds(src_row, 1), :], dst_ref.at[pl.ds(dst_row, 1), :], sem)


def _dispatch_kernel(pos1_ref, pos2_ref, h_ref, xs_init_ref, xs_ref, sem):
    del xs_init_ref
    ts = DISPATCH_TS
    base = pl.program_id(0) * ts

    def issue(r, carry):
        _row_copy(h_ref, r, xs_ref, pos1_ref[base + r], sem).start()
        _row_copy(h_ref, r, xs_ref, pos2_ref[base + r], sem).start()
        return carry

    lax.fori_loop(0, ts, issue, 0, unroll=8)

    def drain(r, carry):
        _row_copy(h_ref, r, xs_ref, 0, sem).wait()
        _row_copy(h_ref, r, xs_ref, 0, sem).wait()
        return carry

    lax.fori_loop(0, ts, drain, 0, unroll=8)


def _dispatch(h2d, pos1, pos2, n_rows):
    n_tok = h2d.shape[0]
    ts = DISPATCH_TS
    assert n_tok % ts == 0
    xs_init = jnp.zeros((n_rows, D_MODEL), F32)
    return pl.pallas_call(
        _dispatch_kernel,
        grid_spec=pltpu.PrefetchScalarGridSpec(
            num_scalar_prefetch=2,
            grid=(n_tok // ts,),
            in_specs=[
                pl.BlockSpec((ts, D_MODEL), lambda i, p1, p2: (i, 0)),
                pl.BlockSpec(memory_space=pl.ANY),
            ],
            out_specs=pl.BlockSpec(memory_space=pl.ANY),
            scratch_shapes=[pltpu.SemaphoreType.DMA],
        ),
        out_shape=jax.ShapeDtypeStruct((n_rows, D_MODEL), F32),
        input_output_aliases={3: 0},
        compiler_params=pltpu.CompilerParams(
            dimension_semantics=("arbitrary",), vmem_limit_bytes=VMEM_LIMIT_BYTES),
        name="moe_dispatch",
    )(pos1, pos2, h2d, xs_init)


def _ffn_moe_kernel(te_ref, tb_ref, tv_ref, x_ref, wg_ref, wu_ref, wd_ref, o_ref, xb_s, acc_s):
    i = pl.program_id(0)
    f = pl.program_id(1)
    valid = tv_ref[i] > 0

    @pl.when(valid & (f == 0))
    def _():
        xb_s[...] = x_ref[...].astype(BF16)
        acc_s[...] = jnp.zeros_like(acc_s)

    @pl.when(valid)
    def _():
        acc_s[...] += _swiglu_partial(xb_s[...], wg_ref[0], wu_ref[0], wd_ref[0])

    @pl.when(valid & (f == pl.num_programs(1) - 1))
    def _():
        o_ref[...] = acc_s[...]


def _ffn_moe(xs, tile_expert, tile_block, tile_valid, wg, wu, wd):
    n_rows = xs.shape[0]
    d_ff = wg.shape[2]
    tm, tf = MOE_TM, MOE_TF
    assert n_rows % tm == 0 and d_ff % tf == 0
    n_f = d_ff // tf
    f_idx = lambda i, f, tv: jnp.where(tv[i] > 0, f, n_f - 1)
    return pl.pallas_call(
        _ffn_moe_kernel,
        grid_spec=pltpu.PrefetchScalarGridSpec(
            num_scalar_prefetch=3,
            grid=(n_rows // tm, n_f),
            in_specs=[
                pl.BlockSpec((tm, D_MODEL), lambda i, f, te, tb, tv: (tb[i], 0)),
                pl.BlockSpec((1, D_MODEL, tf), lambda i, f, te, tb, tv: (te[i], 0, f_idx(i, f, tv))),
                pl.BlockSpec((1, D_MODEL, tf), lambda i, f, te, tb, tv: (te[i], 0, f_idx(i, f, tv))),
                pl.BlockSpec((1, tf, D_MODEL), lambda i, f, te, tb, tv: (te[i], f_idx(i, f, tv), 0)),
            ],
            out_specs=pl.BlockSpec((tm, D_MODEL), lambda i, f, te, tb, tv: (tb[i], 0)),
            scratch_shapes=[pltpu.VMEM((tm, D_MODEL), BF16), pltpu.VMEM((tm, D_MODEL), F32)],
        ),
        out_shape=jax.ShapeDtypeStruct((n_rows, D_MODEL), F32),
        compiler_params=pltpu.CompilerParams(
            dimension_semantics=("arbitrary", "arbitrary"),
            vmem_limit_bytes=VMEM_LIMIT_BYTES),
        name="ffn_moe",
    )(tile_expert, tile_block, tile_valid, xs, wg.astype(BF16), wu.astype(BF16), wd.astype(BF16))


COMBINE_TS = 512


def _combine_kernel(pos1_ref, pos2_ref, h_ref, wts_ref, ys_ref, lg_ref, lb_ref, o_ref, y1_s, y2_s, sem):
    ts = COMBINE_TS
    base = pl.program_id(0) * ts

    def issue(r, carry):
        _row_copy(ys_ref, pos1_ref[base + r], y1_s, r, sem).start()
        _row_copy(ys_ref, pos2_ref[base + r], y2_s, r, sem).start()
        return carry

    lax.fori_loop(0, ts, issue, 0, unroll=8)

    def drain(r, carry):
        _row_copy(ys_ref, 0, y1_s, r, sem).wait()
        _row_copy(ys_ref, 0, y2_s, r, sem).wait()
        return carry

    lax.fori_loop(0, ts, drain, 0, unroll=8)

    w = wts_ref[...]
    f = w[:, 0:1] * y1_s[...] + w[:, 1:2] * y2_s[...]
    o_ref[...] = _layer_norm_rows(ALPHA * h_ref[...] + f, lg_ref[...], lb_ref[...])


def _combine(h2d, wts, ys, pos1, pos2, ln_g, ln_b):
    n_tok = h2d.shape[0]
    ts = COMBINE_TS
    assert n_tok % ts == 0
    row = lambda a: a.reshape(1, -1)
    return pl.pallas_call(
        _combine_kernel,
        grid_spec=pltpu.PrefetchScalarGridSpec(
            num_scalar_prefetch=2,
            grid=(n_tok // ts,),
            in_specs=[
                pl.BlockSpec((ts, D_MODEL), lambda i, p1, p2: (i, 0)),
                pl.BlockSpec((ts, ROUTE_COLS), lambda i, p1, p2: (i, 0)),
                pl.BlockSpec(memory_space=pl.ANY),
                pl.BlockSpec((1, D_MODEL), lambda i, p1, p2: (0, 0)),
                pl.BlockSpec((1, D_MODEL), lambda i, p1, p2: (0, 0)),
            ],
            out_specs=pl.BlockSpec((ts, D_MODEL), lambda i, p1, p2: (i, 0)),
            scratch_shapes=[pltpu.VMEM((ts, D_MODEL), F32), pltpu.VMEM((ts, D_MODEL), F32),
                            pltpu.SemaphoreType.DMA],
        ),
        out_shape=jax.ShapeDtypeStruct((n_tok, D_MODEL), F32),
        compiler_params=pltpu.CompilerParams(
            dimension_semantics=("arbitrary",), vmem_limit_bytes=VMEM_LIMIT_BYTES),
        name="moe_combine",
    )(pos1, pos2, h2d, wts, ys, row(ln_g), row(ln_b))


def _moe_layer(h2d, w_router, b_router, wg, wu, wd, ln_g, ln_b):
    n_tok = h2d.shape[0]
    tm = MOE_TM
    idx, wts, cnt = _route(h2d, w_router, b_router)

    counts = cnt[0, :N_EXPERTS].astype(jnp.int32)
    padded = ((counts + tm - 1) // tm) * tm
    ends = jnp.cumsum(padded)
    starts = ends - padded
    n_tiles = (TOP_K * n_tok) // tm + N_EXPERTS
    tile_row0 = jnp.arange(n_tiles, dtype=jnp.int32) * tm
    tile_valid = (tile_row0 < ends[-1]).astype(jnp.int32)
    last_tile = jnp.maximum(ends[-1] // tm - 1, 0)
    tile_block = jnp.minimum(jnp.arange(n_tiles, dtype=jnp.int32), last_tile)
    tile_expert = jnp.minimum(
        jnp.sum((ends[None, :] <= (tile_block * tm)[:, None]).astype(jnp.int32), axis=1), N_EXPERTS - 1)
    onehot = lambda e: (e[:, None] == jnp.arange(N_EXPERTS, dtype=jnp.int32)[None, :]).astype(jnp.int32)
    pos1 = jnp.sum(onehot(idx[:, 0]) * starts[None, :], axis=1) + idx[:, 2]
    pos2 = jnp.sum(onehot(idx[:, 1]) * starts[None, :], axis=1) + idx[:, 3]

    xs = _dispatch(h2d, pos1, pos2, n_tiles * tm)
    ys = _ffn_moe(xs, tile_expert, tile_block, tile_valid, wg, wu, wd)
    return _combine(h2d, wts, ys, pos1, pos2, ln_g, ln_b)


def kernel(x, w_in, b_gates, conv_w, conv_b, pool_w, pool_scale, mlstm_g, w_out, ln1_g, ln1_b, ln2_g, ln2_b, wg_dense, wu_dense, wd_dense, w_router, b_router, wg_exp, wu_exp, wd_exp):
    bsz, seq, d = x.shape
    x2d = x.reshape(bsz * seq, d)
    for l in range(DEPTH):
        h2d = _mixer_layer(x2d, bsz, seq, w_in[l], b_gates[l], conv_w[l], conv_b[l], pool_w[l],
                           pool_scale[l], mlstm_g[l], w_out[l], ln1_g[l], ln1_b[l])
        j = l // 2
        if l % 2 == 0:
            x2d = _ffn_dense_layer(h2d, wg_dense[j], wu_dense[j], wd_dense[j], ln2_g[l], ln2_b[l])
        else:
            x2d = _moe_layer(h2d, w_router[j], b_router[j], wg_exp[j], wu_exp[j], wd_exp[j],
                             ln2_g[l], ln2_b[l])
    return x2d.reshape(bsz, seq, d)
```

```python
import functools

import jax
import jax.numpy as jnp
from jax import lax
from jax.experimental import pallas as pl
from jax.experimental.pallas import tpu as pltpu

F32 = jnp.float32
BF16 = jnp.bfloat16

D_MODEL = 1024
DEPTH = 4
D_POOL = 512
POOL_WINDOWS = (2, 4, 8, 16)
POOL_GROUP = 128
MAX_WINDOW = max(POOL_WINDOWS)
D_MLSTM = 512
N_HEADS = 4
HEAD_DIM = 128
CONV_WIDTH = 4
N_GATES = 2 * N_HEADS
D_IN = D_POOL + 4 * D_MLSTM + N_GATES
N_EXPERTS = 8
TOP_K = 2
ALPHA = (2 * DEPTH) ** 0.25
LN_EPS = 1e-5

LANES = 128
SUBLANES = 8
D_IN_PAD = D_POOL + 4 * D_MLSTM + LANES
VMEM_LIMIT_BYTES = 56 * 1024 * 1024

OFF_QK = D_POOL
OFF_V = OFF_QK + 2 * D_MLSTM
OFF_O = OFF_V + D_MLSTM
OFF_G = OFF_O + D_MLSTM

MIX_TS = 512
MIX_L = 256
CONV_PAD = SUBLANES
POOL_PAD = 16


def _dot(a, b):
    return jnp.dot(a, b, preferred_element_type=F32)


def _sigmoid(x):
    return 1.0 / (1.0 + jnp.exp(-x))


def _layer_norm_rows(r, g, b):
    mu = jnp.mean(r, axis=-1, keepdims=True)
    rc = r - mu
    var = jnp.mean(rc * rc, axis=-1, keepdims=True)
    return rc * lax.rsqrt(var + LN_EPS) * g + b


def _mixer_kernel(x_ref, win_ref, bg_ref, cw_ref, cb_ref, pw_ref, ps_ref, mg_ref, wout_ref,
                  lg_ref, lb_ref, h_ref,
                  pbuf, cbuf, q_s, k_s, ycat, st_s, m_s):
    ts, L = MIX_TS, MIX_L
    s_idx = pl.program_id(1)

    @pl.when(s_idx == 0)
    def _():
        pbuf[0:POOL_PAD, :] = jnp.zeros((POOL_PAD, D_POOL), F32)
        cbuf[0:CONV_PAD, :] = jnp.zeros((CONV_PAD, 2 * D_MLSTM), F32)
        st_s[...] = jnp.zeros_like(st_s)
        m_s[...] = jnp.zeros_like(m_s)

    x = x_ref[...]
    xb = x.astype(BF16)
    gates = _dot(xb, win_ref[:, OFF_G:D_IN_PAD]) + bg_ref[...]

    g_row = gates.T[0:N_GATES, :]
    logf_row = jnp.minimum(g_row, 0.0) - jnp.log1p(jnp.exp(-jnp.abs(g_row)))
    lane_in_chunk = lax.broadcasted_iota(jnp.int32, (N_GATES, ts), 1) % L
    b_row = logf_row
    sh = 1
    while sh < L:
        b_row = b_row + jnp.where(lane_in_chunk >= sh, pltpu.roll(b_row, sh, axis=1), 0.0)
        sh *= 2
    b_heads = b_row[N_HEADS:N_GATES, :]
    a_row = g_row[0:N_HEADS, :] - b_heads
    cmax_row = a_row
    sh = 1
    while sh < L:
        cmax_row = jnp.maximum(cmax_row, jnp.where(lane_in_chunk[0:N_HEADS] >= sh,
                                                   pltpu.roll(cmax_row, sh, axis=1), -jnp.inf))
        sh *= 2
    cols = jnp.concatenate([cmax_row, b_heads, jnp.zeros((LANES - N_GATES, ts), F32)], axis=0).T

    pbuf[POOL_PAD:POOL_PAD + ts, :] = _dot(xb, win_ref[:, 0:OFF_QK])
    cbuf[CONV_PAD:CONV_PAD + ts, :] = _dot(xb, win_ref[:, OFF_QK:OFF_V])
    v = _dot(xb, win_ref[:, OFF_V:OFF_O]).astype(BF16)
    o_pre = _dot(xb, win_ref[:, OFF_O:OFF_G])

    t_glob = lax.broadcasted_iota(jnp.int32, (ts, 1), 0) + s_idx * ts
    for g, w in enumerate(POOL_WINDOWS):
        cs = slice(g * POOL_GROUP, (g + 1) * POOL_GROUP)
        acc = pbuf[:, cs]
        span = 1
        while span < w:
            acc = acc + pltpu.roll(acc, span, axis=0)
            span *= 2
        acc = acc[POOL_PAD:POOL_PAD + ts, :]
        u = pbuf[POOL_PAD:POOL_PAD + ts, cs]
        cnt = jnp.minimum(t_glob + 1, w).astype(F32)
        d = acc / cnt - u
        y = _dot(d.astype(BF16), pw_ref[g]) * ps_ref[:, cs]
        ycat[:, cs] = y.astype(BF16)
    pbuf[0:POOL_PAD, :] = pbuf[ts:ts + POOL_PAD, :]

    for c in range(2 * D_MLSTM // LANES):
        cs = slice(c * LANES, (c + 1) * LANES)
        y = cb_ref[:, cs] + cbuf[CONV_PAD:CONV_PAD + ts, cs] * cw_ref[CONV_WIDTH - 1:CONV_WIDTH, cs]
        for j in range(1, CONV_WIDTH):
            y = y + cbuf[CONV_PAD - j:CONV_PAD - j + ts, cs] * cw_ref[CONV_WIDTH - 1 - j:CONV_WIDTH - j, cs]
        y = y * _sigmoid(y)
        if c < N_HEADS:
            q_s[:, cs] = y.astype(BF16)
        else:
            ks = slice((c - N_HEADS) * LANES, (c - N_HEADS + 1) * LANES)
            k_s[:, ks] = (y * (HEAD_DIM ** -0.5)).astype(BF16)
    cbuf[0:CONV_PAD, :] = cbuf[ts:ts + CONV_PAD, :]

    tri =(lax.broadcasted_iota(jnp.int32, (L, L), 1) <= lax.broadcasted_iota(jnp.int32, (L, L), 0))
    dn_nt = (((1,), (1,)), ((), ()))
    dn_tn = (((0,), (0,)), ((), ()))
    ones_blk = jnp.ones((L, HEAD_DIM), BF16)

    state = [st_s[h] for h in range(N_HEADS)]
    m_state = [m_s[h:h + 1, 0:1] for h in range(N_HEADS)]
    for c in range(ts // L):
        rs = slice(c * L, (c + 1) * L)
        for h in range(N_HEADS):
            hs = slice(h * HEAD_DIM, (h + 1) * HEAD_DIM)
            cm_c = cols[rs, h:h + 1]
            bc = cols[rs, N_HEADS + h:N_HEADS + h + 1]
            ic = gates[rs, h:h + 1]
            ar = a_row[h:h + 1, rs]
            m_prev = m_state[h]
            q_h = q_s[rs, hs]
            k_h = k_s[rs, hs]
            v2 = jnp.concatenate([v[rs, hs], ones_blk], axis=1)

            mm = jnp.maximum(cm_c, m_prev)
            s_qk = lax.dot_general(q_h, k_h, dn_nt, preferred_element_type=F32)
            wmat = jnp.exp(jnp.where(tri, ar - mm, -jnp.inf)) * s_qk
            s_inter = jnp.exp(m_prev - mm)
            tot = _dot(wmat.astype(BF16), v2) + s_inter * _dot(q_h, state[h].astype(BF16))
            num = tot[:, 0:HEAD_DIM]
            den = tot[:, HEAD_DIM:2 * HEAD_DIM]
            hh = num / jnp.maximum(jnp.abs(den), jnp.exp(-(bc + mm)))

            mu = jnp.mean(hh, axis=1, keepdims=True)
            hc = hh - mu
            var = jnp.mean(hc * hc, axis=1, keepdims=True)
            hn = hc * lax.rsqrt(var + LN_EPS) * mg_ref[:, hs]
            ym = _sigmoid(o_pre[rs, hs]) * hn
            ycat[rs, D_POOL + h * HEAD_DIM:D_POOL + (h + 1) * HEAD_DIM] = ym.astype(BF16)

            g_tot = bc[L - 1:L, :]
            m_loc = g_tot + cm_c[L - 1:L, :]
            wa = jnp.exp(g_tot + ic - bc - m_loc)
            wv2 = (wa * v2.astype(F32)).astype(BF16)
            st_loc = lax.dot_general(k_h, wv2, dn_tn, preferred_element_type=F32)
            m_new = jnp.maximum(g_tot + m_prev, m_loc)
            state[h] = jnp.exp(g_tot + m_prev - m_new) * state[h] + jnp.exp(m_loc - m_new) * st_loc
            m_state[h] = m_new
    for h in range(N_HEADS):
        st_s[h] = state[h]
        m_s[h:h + 1, :] = jnp.broadcast_to(m_state[h], (1, LANES))

    mix = _dot(ycat[...], wout_ref[...])
    h_ref[...] = _layer_norm_rows(ALPHA * x + mix, lg_ref[...], lb_ref[...])


def _mixer_layer(x2d, bsz, seq, w_in, b_gates, conv_w, conv_b, pool_w, pool_scale, mlstm_g, w_out,
                 ln_g, ln_b):
    ts = MIX_TS
    assert seq % ts == 0 and ts % MIX_L == 0
    n_s_tiles = seq // ts
    w_in_p = jnp.pad(w_in, ((0, 0), (0, D_IN_PAD - D_IN))).astype(BF16)
    bg_p = jnp.pad(b_gates, (0, LANES - N_GATES)).reshape(1, LANES)
    row = lambda a: a.reshape(1, -1)
    const2 = lambda b, s: (0, 0)
    const3 = lambda b, s: (0, 0, 0)
    return pl.pallas_call(
        _mixer_kernel,
        grid=(bsz, n_s_tiles),
        in_specs=[
            pl.BlockSpec((ts, D_MODEL), lambda b, s: (b * n_s_tiles + s, 0)),
            pl.BlockSpec((D_MODEL, D_IN_PAD), const2),
            pl.BlockSpec((1, LANES), const2),
            pl.BlockSpec((CONV_WIDTH, 2 * D_MLSTM), const2),
            pl.BlockSpec((1, 2 * D_MLSTM), const2),
            pl.BlockSpec((len(POOL_WINDOWS), POOL_GROUP, POOL_GROUP), const3),
            pl.BlockSpec((1, D_POOL), const2),
            pl.BlockSpec((1, D_MLSTM), const2),
            pl.BlockSpec((D_MODEL, D_MODEL), const2),
            pl.BlockSpec((1, D_MODEL), const2),
            pl.BlockSpec((1, D_MODEL), const2),
        ],
        out_specs=pl.BlockSpec((ts, D_MODEL), lambda b, s: (b * n_s_tiles + s, 0)),
        out_shape=jax.ShapeDtypeStruct((bsz * seq, D_MODEL), F32),
        scratch_shapes=[
            pltpu.VMEM((POOL_PAD + ts, D_POOL), F32),
            pltpu.VMEM((CONV_PAD + ts, 2 * D_MLSTM), F32),
            pltpu.VMEM((ts, D_MLSTM), BF16),
            pltpu.VMEM((ts, D_MLSTM), BF16),
            pltpu.VMEM((ts, D_MODEL), BF16),
            pltpu.VMEM((N_HEADS, HEAD_DIM, 2 * HEAD_DIM), F32),
            pltpu.VMEM((SUBLANES, LANES), F32),
        ],
        compiler_params=pltpu.CompilerParams(
            dimension_semantics=("arbitrary", "arbitrary"),
            vmem_limit_bytes=VMEM_LIMIT_BYTES),
        name="mixer",
    )(x2d, w_in_p, bg_p, conv_w, row(conv_b), pool_w.astype(BF16), row(pool_scale), row(mlstm_g),
      w_out.astype(BF16), row(ln_g), row(ln_b))


DENSE_TM = 512
FF_CHUNK = 512


def _swiglu_rows(x, wg, wu, wd, a_s):
    xb = x.astype(BF16)
    d_ff = a_s.shape[1]
    for c0 in range(0, d_ff, FF_CHUNK):
        cs = slice(c0, min(c0 + FF_CHUNK, d_ff))
        g = _dot(xb, wg[:, cs])
        u = _dot(xb, wu[:, cs])
        a_s[:, cs] = (g * _sigmoid(g) * u).astype(BF16)
    return _dot(a_s[...], wd[...])


def _ffn_dense_kernel(x_ref, wg_ref, wu_ref, wd_ref, lg_ref, lb_ref, o_ref, a_s):
    x = x_ref[...]
    f = _swiglu_rows(x, wg_ref, wu_ref, wd_ref, a_s)
    o_ref[...] = _layer_norm_rows(ALPHA * x + f, lg_ref[...], lb_ref[...])


def _ffn_dense_layer(h2d, wg, wu, wd, ln_g, ln_b):
    n_tok = h2d.shape[0]
    d_ff = wg.shape[1]
    tm = DENSE_TM
    assert n_tok % tm == 0
    row = lambda a: a.reshape(1, -1)
    resident = dict(pipeline_mode=pl.Buffered(1))
    return pl.pallas_call(
        _ffn_dense_kernel,
        grid=(n_tok // tm,),
        in_specs=[
            pl.BlockSpec((tm, D_MODEL), lambda i: (i, 0)),
            pl.BlockSpec((D_MODEL, d_ff), lambda i: (0, 0), **resident),
            pl.BlockSpec((D_MODEL, d_ff), lambda i: (0, 0), **resident),
            pl.BlockSpec((d_ff, D_MODEL), lambda i: (0, 0), **resident),
            pl.BlockSpec((1, D_MODEL), lambda i: (0, 0)),
            pl.BlockSpec((1, D_MODEL), lambda i: (0, 0)),
        ],
        out_specs=pl.BlockSpec((tm, D_MODEL), lambda i: (i, 0)),
        out_shape=jax.ShapeDtypeStruct((n_tok, D_MODEL), F32),
        scratch_shapes=[pltpu.VMEM((tm, d_ff), BF16)],
        compiler_params=pltpu.CompilerParams(
            dimension_semantics=("arbitrary",), vmem_limit_bytes=VMEM_LIMIT_BYTES),
        name="ffn_dense",
    )(h2d, wg.astype(BF16), wu.astype(BF16), wd.astype(BF16), row(ln_g), row(ln_b))


ROUTE_TS = 512
DISPATCH_TS = 512
MOE_TM = 512
ROUTE_COLS = 8


def _route_kernel(h_ref, wr_ref, br_ref, idx_ref, wts_ref, cnt_ref, carry_s):
    ts = ROUTE_TS
    i = pl.program_id(0)

    @pl.when(i == 0)
    def _():
        carry_s[...] = jnp.zeros_like(carry_s)

    logits = jnp.dot(h_ref[...], wr_ref[...], precision=lax.Precision.HIGHEST,
                     preferred_element_type=F32) + br_ref[...]
    lane = lax.broadcasted_iota(jnp.int32, (ts, LANES), 1)
    lg = jnp.where(lane < N_EXPERTS, logits, -jnp.inf)
    m1 = jnp.max(lg, axis=1, keepdims=True)
    i1 = jnp.min(jnp.where(lg == m1, lane, LANES), axis=1, keepdims=True)
    oh1 = lane == i1
    lg2 = jnp.where(oh1, -jnp.inf, lg)
    m2 = jnp.max(lg2, axis=1, keepdims=True)
    i2 = jnp.min(jnp.where(lg2 == m2, lane, LANES), axis=1, keepdims=True)
    oh2 = lane == i2
    e2 = jnp.exp(m2 - m1)
    w1 = 1.0 / (1.0 + e2)
    w2 = e2 / (1.0 + e2)

    sel = jnp.where(oh1 | oh2, 1.0, 0.0)
    strict_lower = jnp.where(lax.broadcasted_iota(jnp.int32, (ts, ts), 1)
                             < lax.broadcasted_iota(jnp.int32, (ts, ts), 0), 1.0, 0.0).astype(BF16)
    before = _dot(strict_lower, sel.astype(BF16)) + carry_s[...]
    r1 = jnp.sum(jnp.where(oh1, before, 0.0), axis=1, keepdims=True).astype(jnp.int32)
    r2 = jnp.sum(jnp.where(oh2, before, 0.0), axis=1, keepdims=True).astype(jnp.int32)
    carry_s[...] += jnp.sum(sel, axis=0, keepdims=True)

    col = lax.broadcasted_iota(jnp.int32, (ts, ROUTE_COLS), 1)
    idx_ref[...] = jnp.where(col == 0, i1, jnp.where(col == 1, i2, jnp.where(col == 2, r1, r2)))
    wts_ref[...] = jnp.where(col == 0, w1, w2)
    cnt_ref[...] = carry_s[...]


def _route(h2d, w_router, b_router):
    n_tok = h2d.shape[0]
    ts = ROUTE_TS
    assert n_tok % ts == 0
    wr = jnp.pad(w_router, ((0, 0), (0, LANES - N_EXPERTS)))
    br = jnp.pad(b_router, (0, LANES - N_EXPERTS)).reshape(1, LANES)
    return pl.pallas_call(
        _route_kernel,
        grid=(n_tok // ts,),
        in_specs=[
            pl.BlockSpec((ts, D_MODEL), lambda i: (i, 0)),
            pl.BlockSpec((D_MODEL, LANES), lambda i: (0, 0)),
            pl.BlockSpec((1, LANES), lambda i: (0, 0)),
        ],
        out_specs=[
            pl.BlockSpec((ts, ROUTE_COLS), lambda i: (i, 0)),
            pl.BlockSpec((ts, ROUTE_COLS), lambda i: (i, 0)),
            pl.BlockSpec((1, LANES), lambda i: (0, 0)),
        ],
        out_shape=[
            jax.ShapeDtypeStruct((n_tok, ROUTE_COLS), jnp.int32),
            jax.ShapeDtypeStruct((n_tok, ROUTE_COLS), F32),
            jax.ShapeDtypeStruct((1, LANES), F32),
        ],
        scratch_shapes=[pltpu.VMEM((1, LANES), F32)],
        compiler_params=pltpu.CompilerParams(
            dimension_semantics=("arbitrary",), vmem_limit_bytes=VMEM_LIMIT_BYTES),
        name="moe_route",
    )(h2d, wr, br)


def _row_copy(src_ref, src_row, dst_ref, dst_row, sem):
    return pltpu.make_async_copy(src_ref.at[pl.ds(src_row, 1), :], dst_ref.at[pl.ds(dst_row, 1), :], sem)


def _dispatch_kernel(pos1_ref, pos2_ref, h_ref, xs_init_ref, xs_ref, sem):
    del xs_init_ref
    ts = DISPATCH_TS
    base = pl.program_id(0) * ts

    def issue(r, carry):
        _row_copy(h_ref, r, xs_ref, pos1_ref[base + r], sem).start()
        _row_copy(h_ref, r, xs_ref, pos2_ref[base + r], sem).start()
        return carry

    lax.fori_loop(0, ts, issue, 0, unroll=8)

    def drain(r, carry):
        _row_copy(h_ref, r, xs_ref, 0, sem).wait()
        _row_copy(h_ref, r, xs_ref, 0, sem).wait()
        return carry

    lax.fori_loop(0, ts, drain, 0, unroll=8)


def _dispatch(h2d, pos1, pos2, n_rows):
    n_tok = h2d.shape[0]
    ts = DISPATCH_TS
    assert n_tok % ts == 0
    xs_init = jnp.zeros((n_rows, D_MODEL), F32)
    return pl.pallas_call(
        _dispatch_kernel,
        grid_spec=pltpu.PrefetchScalarGridSpec(
            num_scalar_prefetch=2,
            grid=(n_tok // ts,),
            in_specs=[
                pl.BlockSpec((ts, D_MODEL), lambda i, p1, p2: (i, 0)),
                pl.BlockSpec(memory_space=pl.ANY),
            ],
            out_specs=pl.BlockSpec(memory_space=pl.ANY),
            scratch_shapes=[pltpu.SemaphoreType.DMA],
        ),
        out_shape=jax.ShapeDtypeStruct((n_rows, D_MODEL), F32),
        input_output_aliases={3: 0},
        compiler_params=pltpu.CompilerParams(
            dimension_semantics=("arbitrary",), vmem_limit_bytes=VMEM_LIMIT_BYTES),
        name="moe_dispatch",
    )(pos1, pos2, h2d, xs_init)


def _ffn_moe_kernel(te_ref, tb_ref, tv_ref, x_ref, wg_ref, wu_ref, wd_ref, o_ref, a_s):
    del te_ref, tb_ref

    @pl.when(tv_ref[pl.program_id(0)] > 0)
    def _():
        o_ref[...] = _swiglu_rows(x_ref[...], wg_ref.at[0], wu_ref.at[0], wd_ref.at[0], a_s)


def _ffn_moe(xs, tile_expert, tile_block, tile_valid, wg, wu, wd):
    n_rows = xs.shape[0]
    d_ff = wg.shape[2]
    tm = MOE_TM
    assert n_rows % tm == 0
    resident = dict(pipeline_mode=pl.Buffered(1))
    return pl.pallas_call(
        _ffn_moe_kernel,
        grid_spec=pltpu.PrefetchScalarGridSpec(
            num_scalar_prefetch=3,
            grid=(n_rows // tm,),
            in_specs=[
                pl.BlockSpec((tm, D_MODEL), lambda i, te, tb, tv: (tb[i], 0)),
                pl.BlockSpec((1, D_MODEL, d_ff), lambda i, te, tb, tv: (te[i], 0, 0), **resident),
                pl.BlockSpec((1, D_MODEL, d_ff), lambda i, te, tb, tv: (te[i], 0, 0), **resident),
                pl.BlockSpec((1, d_ff, D_MODEL), lambda i, te, tb, tv: (te[i], 0, 0), **resident),
            ],
            out_specs=pl.BlockSpec((tm, D_MODEL), lambda i, te, tb, tv: (tb[i], 0)),
            scratch_shapes=[pltpu.VMEM((tm, d_ff), BF16)],
        ),
        out_shape=jax.ShapeDtypeStruct((n_rows, D_MODEL), F32),
        compiler_params=pltpu.CompilerParams(
            dimension_semantics=("arbitrary",), vmem_limit_bytes=VMEM_LIMIT_BYTES),
        name="ffn_moe",
    )(tile_expert, tile_block, tile_valid, xs, wg.astype(BF16), wu.astype(BF16), wd.astype(BF16))


COMBINE_TS = 512


def _combine_kernel(pos1_ref, pos2_ref, h_ref, wts_ref, ys_ref, lg_ref, lb_ref, o_ref, y1_s, y2_s, sem):
    ts = COMBINE_TS
    base = pl.program_id(0) * ts

    def issue(r, carry):
        _row_copy(ys_ref, pos1_ref[base + r], y1_s, r, sem).start()
        _row_copy(ys_ref, pos2_ref[base + r], y2_s, r, sem).start()
        return carry

    lax.fori_loop(0, ts, issue, 0, unroll=8)

    def drain(r, carry):
        _row_copy(ys_ref, 0, y1_s, r, sem).wait()
        _row_copy(ys_ref, 0, y2_s, r, sem).wait()
        return carry

    lax.fori_loop(0, ts, drain, 0, unroll=8)

    w = wts_ref[...]
    f = w[:, 0:1] * y1_s[...] + w[:, 1:2] * y2_s[...]
    o_ref[...] = _layer_norm_rows(ALPHA * h_ref[...] + f, lg_ref[...], lb_ref[...])


def _combine(h2d, wts, ys, pos1, pos2, ln_g, ln_b):
    n_tok = h2d.shape[0]
    ts = COMBINE_TS
    assert n_tok % ts == 0
    row = lambda a: a.reshape(1, -1)
    return pl.pallas_call(
        _combine_kernel,
        grid_spec=pltpu.PrefetchScalarGridSpec(
            num_scalar_prefetch=2,
            grid=(n_tok // ts,),
            in_specs=[
                pl.BlockSpec((ts, D_MODEL), lambda i, p1, p2: (i, 0)),
                pl.BlockSpec((ts, ROUTE_COLS), lambda i, p1, p2: (i, 0)),
                pl.BlockSpec(memory_space=pl.ANY),
                pl.BlockSpec((1, D_MODEL), lambda i, p1, p2: (0, 0)),
                pl.BlockSpec((1, D_MODEL), lambda i, p1, p2: (0, 0)),
            ],
            out_specs=pl.BlockSpec((ts, D_MODEL), lambda i, p1, p2: (i, 0)),
            scratch_shapes=[pltpu.VMEM((ts, D_MODEL), F32), pltpu.VMEM((ts, D_MODEL), F32),
                            pltpu.SemaphoreType.DMA],
        ),
        out_shape=jax.ShapeDtypeStruct((n_tok, D_MODEL), F32),
        compiler_params=pltpu.CompilerParams(
            dimension_semantics=("arbitrary",), vmem_limit_bytes=VMEM_LIMIT_BYTES),
        name="moe_combine",
    )(pos1, pos2, h2d, wts, ys, row(ln_g), row(ln_b))


def _moe_layer(h2d, w_router, b_router, wg, wu, wd, ln_g, ln_b):
    n_tok = h2d.shape[0]
    tm = MOE_TM
    idx, wts, cnt = _route(h2d, w_router, b_router)

    counts = cnt[0, :N_EXPERTS].astype(jnp.int32)
    padded = ((counts + tm - 1) // tm) * tm
    ends = jnp.cumsum(padded)
    starts = ends - padded
    n_tiles = (TOP_K * n_tok) // tm + N_EXPERTS
    tile_row0 = jnp.arange(n_tiles, dtype=jnp.int32) * tm
    tile_valid = (tile_row0 < ends[-1]).astype(jnp.int32)
    last_tile = jnp.maximum(ends[-1] // tm - 1, 0)
    tile_block = jnp.minimum(jnp.arange(n_tiles, dtype=jnp.int32), last_tile)
    tile_expert = jnp.minimum(
        jnp.sum((ends[None, :] <= (tile_block * tm)[:, None]).astype(jnp.int32), axis=1), N_EXPERTS - 1)
    onehot = lambda e: (e[:, None] == jnp.arange(N_EXPERTS, dtype=jnp.int32)[None, :]).astype(jnp.int32)
    pos1 = jnp.sum(onehot(idx[:, 0]) * starts[None, :], axis=1) + idx[:, 2]
    pos2 = jnp.sum(onehot(idx[:, 1]) * starts[None, :], axis=1) + idx[:, 3]

    xs = _dispatch(h2d, pos1, pos2, n_tiles * tm)
    ys = _ffn_moe(xs, tile_expert, tile_block, tile_valid, wg, wu, wd)
    return _combine(h2d, wts, ys, pos1, pos2, ln_g, ln_b)


def kernel(x, w_in, b_gates, conv_w, conv_b, pool_w, pool_scale, mlstm_g, w_out, ln1_g, ln1_b, ln2_g, ln2_b, wg_dense, wu_dense, wd_dense, w_router, b_router, wg_exp, wu_exp, wd_exp):
    bsz, seq, d = x.shape
    x2d = x.reshape(bsz * seq, d)
    for l in range(DEPTH):
        h2d = _mixer_layer(x2d, bsz, seq, w_in[l], b_gates[l], conv_w[l], conv_b[l], pool_w[l],
                           pool_scale[l], mlstm_g[l], w_out[l], ln1_g[l], ln1_b[l])
        j = l // 2
        if l % 2 == 0:
            x2d = _ffn_dense_layer(h2d, wg_dense[j], wu_dense[j], wd_dense[j], ln2_g[l], ln2_b[l])
        else:
            x2d = _moe_layer(h2d, w_router[j], b_router[j], wg_exp[j], wu_exp[j], wd_exp[j],
                             ln2_g[l], ln2_b[l])
    return x2d.reshape(bsz, seq, d)
```

```python
import functools

import jax
import jax.numpy as jnp
from jax import lax
from jax.experimental import pallas as pl
from jax.experimental.pallas import tpu as pltpu

F32 = jnp.float32
BF16 = jnp.bfloat16

D_MODEL = 1024
DEPTH = 4
D_POOL = 512
POOL_WINDOWS = (2, 4, 8, 16)
POOL_GROUP = 128
MAX_WINDOW = max(POOL_WINDOWS)
D_MLSTM = 512
N_HEADS = 4
HEAD_DIM = 128
CONV_WIDTH = 4
N_GATES = 2 * N_HEADS
D_IN = D_POOL + 4 * D_MLSTM + N_GATES
N_EXPERTS = 8
TOP_K = 2
ALPHA = (2 * DEPTH) ** 0.25
LN_EPS = 1e-5

LANES = 128
SUBLANES = 8
D_IN_PAD = D_POOL + 4 * D_MLSTM + LANES
VMEM_LIMIT_BYTES = 56 * 1024 * 1024

OFF_QK = D_POOL
OFF_V = OFF_QK + 2 * D_MLSTM
OFF_O = OFF_V + D_MLSTM
OFF_G = OFF_O + D_MLSTM

MIX_TS = 512
MIX_L = 256
CONV_PAD = SUBLANES
POOL_PAD = 16


def _dot(a, b):
    return jnp.dot(a, b, preferred_element_type=F32)


def _sigmoid(x):
    return 1.0 / (1.0 + jnp.exp(-x))


def _layer_norm_rows(r, g, b):
    mu = jnp.mean(r, axis=-1, keepdims=True)
    rc = r - mu
    var = jnp.mean(rc * rc, axis=-1, keepdims=True)
    return rc * lax.rsqrt(var + LN_EPS) * g + b


def _mixer_kernel(x_ref, win_ref, bg_ref, cw_ref, cb_ref, pw_ref, ps_ref, mg_ref, wout_ref,
                  lg_ref, lb_ref, h_ref,
                  pbuf, cbuf, q_s, k_s, ycat, st_s, m_s):
    ts, L = MIX_TS, MIX_L
    s_idx = pl.program_id(1)

    @pl.when(s_idx == 0)
    def _():
        pbuf[0:POOL_PAD, :] = jnp.zeros((POOL_PAD, D_POOL), F32)
        cbuf[0:CONV_PAD, :] = jnp.zeros((CONV_PAD, 2 * D_MLSTM), F32)
        st_s[...] = jnp.zeros_like(st_s)
        m_s[...] = jnp.zeros_like(m_s)

    xb = x_ref[...].astype(BF16)
    gates =_dot(xb, win_ref[:, OFF_G:D_IN_PAD]) + bg_ref[...]

    g_row = gates.T[0:N_GATES, :]
    logf_row = jnp.minimum(g_row, 0.0) - jnp.log1p(jnp.exp(-jnp.abs(g_row)))
    lane_in_chunk = lax.broadcasted_iota(jnp.int32, (N_GATES, ts), 1) % L
    b_row = logf_row
    sh = 1
    while sh < L:
        b_row = b_row + jnp.where(lane_in_chunk >= sh, pltpu.roll(b_row, sh, axis=1), 0.0)
        sh *= 2
    b_heads = b_row[N_HEADS:N_GATES, :]
    a_row = g_row[0:N_HEADS, :] - b_heads
    cmax_row = a_row
    sh = 1
    while sh < L:
        cmax_row = jnp.maximum(cmax_row, jnp.where(lane_in_chunk[0:N_HEADS] >= sh,
                                                   pltpu.roll(cmax_row, sh, axis=1), -jnp.inf))
        sh *= 2
    cols = jnp.concatenate([cmax_row, b_heads, jnp.zeros((LANES - N_GATES, ts), F32)], axis=0).T

    cbuf[CONV_PAD:CONV_PAD + ts, :] = _dot(xb, win_ref[:, OFF_QK:OFF_V])
    pbuf[POOL_PAD:POOL_PAD + ts, :] = _dot(xb, win_ref[:, 0:OFF_QK])
    v = _dot(xb, win_ref[:, OFF_V:OFF_O]).astype(BF16)
    o_pre = _dot(xb, win_ref[:, OFF_O:OFF_G])

    for c in range(2 * D_MLSTM // LANES):
        cs = slice(c * LANES, (c + 1) * LANES)
        y = cb_ref[:, cs] + cbuf[CONV_PAD:CONV_PAD + ts, cs] * cw_ref[CONV_WIDTH - 1:CONV_WIDTH, cs]
        for j in range(1, CONV_WIDTH):
            y = y + cbuf[CONV_PAD - j:CONV_PAD - j + ts, cs] * cw_ref[CONV_WIDTH - 1 - j:CONV_WIDTH - j, cs]
        y = y * _sigmoid(y)
        if c < N_HEADS:
            q_s[:, cs] = y.astype(BF16)
        else:
            ks = slice((c - N_HEADS) * LANES, (c - N_HEADS + 1) * LANES)
            k_s[:, ks] = (y * (HEAD_DIM ** -0.5)).astype(BF16)
    cbuf[0:CONV_PAD, :] = cbuf[ts:ts + CONV_PAD, :]

    t_glob = lax.broadcasted_iota(jnp.int32, (ts, 1), 0) + s_idx * ts
    for g, w in enumerate(POOL_WINDOWS):
        cs = slice(g * POOL_GROUP, (g + 1) * POOL_GROUP)
        acc = pbuf[:, cs]
        span = 1
        while span < w:
            acc = acc + pltpu.roll(acc, span, axis=0)
            span *= 2
        acc = acc[POOL_PAD:POOL_PAD + ts, :]
        u = pbuf[POOL_PAD:POOL_PAD + ts, cs]
        cnt = jnp.minimum(t_glob + 1, w).astype(F32)
        d = acc / cnt - u
        y = _dot(d.astype(BF16), pw_ref[g]) * ps_ref[:, cs]
        ycat[:, cs] = y.astype(BF16)
    pbuf[0:POOL_PAD, :] = pbuf[ts:ts + POOL_PAD, :]

    tri = (lax.broadcasted_iota(jnp.int32, (L, L), 1) <= lax.broadcasted_iota(jnp.int32, (L, L), 0))
    dn_nt = (((1,), (1,)), ((), ()))
    dn_tn = (((0,), (0,)), ((), ()))
    ones_blk = jnp.ones((L, HEAD_DIM), BF16)

    state = [st_s[h] for h in range(N_HEADS)]
    m_state = [m_s[h:h + 1, 0:1] for h in range(N_HEADS)]
    for c in range(ts // L):
        rs = slice(c * L, (c + 1) * L)
        for h in range(N_HEADS):
            hs = slice(h * HEAD_DIM, (h + 1) * HEAD_DIM)
            cm_c = cols[rs, h:h + 1]
            bc = cols[rs, N_HEADS + h:N_HEADS + h + 1]
            ic = gates[rs, h:h + 1]
            ar = a_row[h:h + 1, rs]
            m_prev = m_state[h]
            q_h = q_s[rs, hs]
            k_h = k_s[rs, hs]
            v2 = jnp.concatenate([v[rs, hs], ones_blk], axis=1)

            mm = jnp.maximum(cm_c, m_prev)
            s_qk = lax.dot_general(q_h, k_h, dn_nt, preferred_element_type=F32)
            wmat = jnp.exp(jnp.where(tri, ar - mm, -jnp.inf)) * s_qk
            s_inter = jnp.exp(m_prev - mm)
            tot = _dot(wmat.astype(BF16), v2) + s_inter * _dot(q_h, state[h].astype(BF16))
            num = tot[:, 0:HEAD_DIM]
            den = tot[:, HEAD_DIM:2 * HEAD_DIM]
            hh = num / jnp.maximum(jnp.abs(den), jnp.exp(-(bc + mm)))

            mu = jnp.mean(hh, axis=1, keepdims=True)
            hc = hh - mu
            var = jnp.mean(hc * hc, axis=1, keepdims=True)
            hn = hc * lax.rsqrt(var + LN_EPS) * mg_ref[:, hs]
            ym = _sigmoid(o_pre[rs, hs]) * hn
            ycat[rs, D_POOL + h * HEAD_DIM:D_POOL + (h + 1) * HEAD_DIM] = ym.astype(BF16)

            g_tot = bc[L - 1:L, :]
            m_loc = g_tot + cm_c[L - 1:L, :]
            wa = jnp.exp(g_tot + ic - bc - m_loc)
            wv2 = (wa * v2.astype(F32)).astype(BF16)
            st_loc = lax.dot_general(k_h, wv2, dn_tn, preferred_element_type=F32)
            m_new = jnp.maximum(g_tot + m_prev, m_loc)
            state[h] = jnp.exp(g_tot + m_prev - m_new) * state[h] + jnp.exp(m_loc - m_new) * st_loc
            m_state[h] = m_new
        mix = _dot(ycat[rs, :], wout_ref[...])
        h_ref[rs, :] = _layer_norm_rows(ALPHA * x_ref[rs, :] + mix, lg_ref[...], lb_ref[...])
    for h in range(N_HEADS):
        st_s[h] = state[h]
        m_s[h:h + 1, :] = jnp.broadcast_to(m_state[h], (1, LANES))


def _mixer_layer(x2d, bsz, seq, w_in, b_gates, conv_w, conv_b, pool_w, pool_scale, mlstm_g, w_out,
                 ln_g, ln_b):
    ts = MIX_TS
    assert seq % ts == 0 and ts % MIX_L == 0
    n_s_tiles = seq // ts
    w_in_p = jnp.pad(w_in, ((0, 0), (0, D_IN_PAD - D_IN))).astype(BF16)
    bg_p = jnp.pad(b_gates, (0, LANES - N_GATES)).reshape(1, LANES)
    row = lambda a: a.reshape(1, -1)
    const2 = lambda b, s: (0, 0)
    const3 = lambda b, s: (0, 0, 0)
    return pl.pallas_call(
        _mixer_kernel,
        grid=(bsz, n_s_tiles),
        in_specs=[
            pl.BlockSpec((ts, D_MODEL), lambda b, s: (b * n_s_tiles + s, 0)),
            pl.BlockSpec((D_MODEL, D_IN_PAD), const2),
            pl.BlockSpec((1, LANES), const2),
            pl.BlockSpec((CONV_WIDTH, 2 * D_MLSTM), const2),
            pl.BlockSpec((1, 2 * D_MLSTM), const2),
            pl.BlockSpec((len(POOL_WINDOWS), POOL_GROUP, POOL_GROUP), const3),
            pl.BlockSpec((1, D_POOL), const2),
            pl.BlockSpec((1, D_MLSTM), const2),
            pl.BlockSpec((D_MODEL, D_MODEL), const2),
            pl.BlockSpec((1, D_MODEL), const2),
            pl.BlockSpec((1, D_MODEL), const2),
        ],
        out_specs=pl.BlockSpec((ts, D_MODEL), lambda b, s: (b * n_s_tiles + s, 0)),
        out_shape=jax.ShapeDtypeStruct((bsz * seq, D_MODEL), F32),
        scratch_shapes=[
            pltpu.VMEM((POOL_PAD + ts, D_POOL), F32),
            pltpu.VMEM((CONV_PAD + ts, 2 * D_MLSTM), F32),
            pltpu.VMEM((ts, D_MLSTM), BF16),
            pltpu.VMEM((ts, D_MLSTM), BF16),
            pltpu.VMEM((ts, D_MODEL), BF16),
            pltpu.VMEM((N_HEADS, HEAD_DIM, 2 * HEAD_DIM), F32),
            pltpu.VMEM((SUBLANES, LANES), F32),
        ],
        compiler_params=pltpu.CompilerParams(
            dimension_semantics=("arbitrary", "arbitrary"),
            vmem_limit_bytes=VMEM_LIMIT_BYTES),
        name="mixer",
    )(x2d, w_in_p, bg_p, conv_w, row(conv_b), pool_w.astype(BF16), row(pool_scale), row(mlstm_g),
      w_out.astype(BF16), row(ln_g), row(ln_b))


DENSE_TM = 512
FF_CHUNK = 512


def _swiglu_hidden(xb, wg, wu, a_s):
    d_ff = a_s.shape[1]
    for c0 in range(0, d_ff, FF_CHUNK):
        cs = slice(c0, min(c0 + FF_CHUNK, d_ff))
        g = _dot(xb, wg[:, cs])
        u = _dot(xb, wu[:, cs])
        a_s[:, cs] = (g * _sigmoid(g) * u).astype(BF16)


def _ffn_dense_kernel(x_ref, wg_ref, wu_ref, wd_ref, lg_ref, lb_ref, o_ref, a_s):
    x = x_ref[...]
    _swiglu_hidden(x.astype(BF16), wg_ref, wu_ref, a_s)
    f = _dot(a_s[...], wd_ref[...])
    o_ref[...] = _layer_norm_rows(ALPHA * x + f, lg_ref[...], lb_ref[...])


def _ffn_dense_layer(h2d, wg, wu, wd, ln_g, ln_b):
    n_tok = h2d.shape[0]
    d_ff = wg.shape[1]
    tm = DENSE_TM
    assert n_tok % tm == 0
    row = lambda a: a.reshape(1, -1)
    resident = dict(pipeline_mode=pl.Buffered(1))
    return pl.pallas_call(
        _ffn_dense_kernel,
        grid=(n_tok // tm,),
        in_specs=[
            pl.BlockSpec((tm, D_MODEL), lambda i: (i, 0)),
            pl.BlockSpec((D_MODEL, d_ff), lambda i: (0, 0), **resident),
            pl.BlockSpec((D_MODEL, d_ff), lambda i: (0, 0), **resident),
            pl.BlockSpec((d_ff, D_MODEL), lambda i: (0, 0), **resident),
            pl.BlockSpec((1, D_MODEL), lambda i: (0, 0)),
            pl.BlockSpec((1, D_MODEL), lambda i: (0, 0)),
        ],
        out_specs=pl.BlockSpec((tm, D_MODEL), lambda i: (i, 0)),
        out_shape=jax.ShapeDtypeStruct((n_tok, D_MODEL), F32),
        scratch_shapes=[pltpu.VMEM((tm, d_ff), BF16)],
        compiler_params=pltpu.CompilerParams(
            dimension_semantics=("arbitrary",), vmem_limit_bytes=VMEM_LIMIT_BYTES),
        name="ffn_dense",
    )(h2d, wg.astype(BF16), wu.astype(BF16), wd.astype(BF16), row(ln_g), row(ln_b))


ROUTE_TS = 512
MOE_TM = 512
ROUTE_COLS = 8
INVERT_STEPS = 16


def _route_kernel(h_ref, wr_ref, br_ref, idx_ref, wts_ref, cnt_ref, carry_s):
    ts = ROUTE_TS
    i = pl.program_id(0)

    @pl.when(i == 0)
    def _():
        carry_s[...] = jnp.zeros_like(carry_s)

    logits = jnp.dot(h_ref[...], wr_ref[...], precision=lax.Precision.HIGHEST,
                     preferred_element_type=F32) + br_ref[...]
    lane = lax.broadcasted_iota(jnp.int32, (ts, LANES), 1)
    lg = jnp.where(lane < N_EXPERTS, logits, -jnp.inf)
    m1 = jnp.max(lg, axis=1, keepdims=True)
    i1 = jnp.min(jnp.where(lg == m1, lane, LANES), axis=1, keepdims=True)
    oh1 = lane == i1
    lg2 = jnp.where(oh1, -jnp.inf, lg)
    m2 = jnp.max(lg2, axis=1, keepdims=True)
    i2 = jnp.min(jnp.where(lg2 == m2, lane, LANES), axis=1, keepdims=True)
    oh2 = lane == i2
    e2 = jnp.exp(m2 - m1)
    w1 = 1.0 / (1.0 + e2)
    w2 = e2 / (1.0 + e2)

    sel = jnp.where(oh1 | oh2, 1.0, 0.0)
    strict_lower = jnp.where(lax.broadcasted_iota(jnp.int32, (ts, ts), 1)
                             < lax.broadcasted_iota(jnp.int32, (ts, ts), 0), 1.0, 0.0).astype(BF16)
    before = _dot(strict_lower, sel.astype(BF16)) + carry_s[...]
    r1 = jnp.sum(jnp.where(oh1, before, 0.0), axis=1, keepdims=True).astype(jnp.int32)
    r2 = jnp.sum(jnp.where(oh2, before, 0.0), axis=1, keepdims=True).astype(jnp.int32)
    carry_s[...] += jnp.sum(sel, axis=0, keepdims=True)

    col = lax.broadcasted_iota(jnp.int32, (ts, ROUTE_COLS), 1)
    idx_ref[...] = jnp.where(col == 0, i1, jnp.where(col == 1, i2, jnp.where(col == 2, r1, r2)))
    wts_ref[...] = jnp.where(col == 0, w1, w2)
    cnt_ref[...] = carry_s[...]


def _route(h2d, w_router, b_router):
    n_tok = h2d.shape[0]
    ts = ROUTE_TS
    assert n_tok % ts == 0
    wr = jnp.pad(w_router, ((0, 0), (0, LANES - N_EXPERTS)))
    br = jnp.pad(b_router, (0, LANES - N_EXPERTS)).reshape(1, LANES)
    return pl.pallas_call(
        _route_kernel,
        grid=(n_tok // ts,),
        in_specs=[
            pl.BlockSpec((ts, D_MODEL), lambda i: (i, 0)),
            pl.BlockSpec((D_MODEL, LANES), lambda i: (0, 0)),
            pl.BlockSpec((1, LANES), lambda i: (0, 0)),
        ],
        out_specs=[
            pl.BlockSpec((ts, ROUTE_COLS), lambda i: (i, 0)),
            pl.BlockSpec((ts, ROUTE_COLS), lambda i: (i, 0)),
            pl.BlockSpec((1, LANES), lambda i: (0, 0)),
        ],
        out_shape=[
            jax.ShapeDtypeStruct((n_tok, ROUTE_COLS), jnp.int32),
            jax.ShapeDtypeStruct((n_tok, ROUTE_COLS), F32),
            jax.ShapeDtypeStruct((1, LANES), F32),
        ],
        scratch_shapes=[pltpu.VMEM((1, LANES), F32)],
        compiler_params=pltpu.CompilerParams(
            dimension_semantics=("arbitrary",), vmem_limit_bytes=VMEM_LIMIT_BYTES),
        name="moe_route",
    )(h2d, wr, br)


def _row_copy(src_ref, src_row, dst_ref, dst_row, sem):
    return pltpu.make_async_copy(src_ref.at[pl.ds(src_row, 1), :], dst_ref.at[pl.ds(dst_row, 1), :], sem)


def _invert_kernel(pos1_ref, pos2_ref, code_ref):
    n_tok = pos1_ref.shape[0]
    n_code = code_ref.shape[0]
    s = pl.program_id(0)
    codes_per_step = n_code // INVERT_STEPS
    toks_per_step = n_tok // INVERT_STEPS

    @pl.when(s < INVERT_STEPS)
    def _():
        base = s * codes_per_step

        def init(j, carry):
            code_ref[base + j] = 2 * n_tok + base + j
            return carry

        lax.fori_loop(0, codes_per_step, init, 0, unroll=8)

    @pl.when(s >= INVERT_STEPS)
    def _():
        base = (s - INVERT_STEPS) * toks_per_step

        def scatter(j, carry):
            t = base + j
            code_ref[pos1_ref[t]] = t
            code_ref[pos2_ref[t]] = n_tok + t
            return carry

        lax.fori_loop(0, toks_per_step, scatter, 0, unroll=8)


def _invert(pos1, pos2, n_code):
    assert n_code % INVERT_STEPS == 0 and pos1.shape[0] % INVERT_STEPS == 0
    smem = pl.BlockSpec(memory_space=pltpu.SMEM)
    return pl.pallas_call(
        _invert_kernel,
        grid=(2 * INVERT_STEPS,),
        in_specs=[smem, smem],
        out_specs=smem,
        out_shape=jax.ShapeDtypeStruct((n_code,), jnp.int32),
        compiler_params=pltpu.CompilerParams(dimension_semantics=("arbitrary",)),
        name="moe_invert",
    )(pos1, pos2)


def _ffn_moe_kernel(te_ref, nu_ref, code_ref, h_hbm, wg_ref, wu_ref, wd_ref, y_hbm,
                    xbuf, xb_s, a_s, obuf, gsem, ssem):
    del te_ref
    tm = MOE_TM
    n_tok = h_hbm.shape[0]
    i = pl.program_id(0)
    n_used = nu_ref[0]

    def gather_copy(r, code):
        tok = code & (n_tok - 1) if n_tok & (n_tok - 1) == 0 else lax.rem(code, n_tok)
        return _row_copy(h_hbm, tok, xbuf, r, gsem)

    def scatter_copy(r, code):
        return _row_copy(obuf, r, y_hbm, code, ssem)

    def issue_inline(make_copy, tile):
        for r in range(tm):
            make_copy(r, code_ref[tile * tm + r]).start(priority=r % 2)

    def issue_loop(make_copy, tile):
        def body(r, carry):
            make_copy(r, code_ref[tile * tm + r]).start()
            return carry

        lax.fori_loop(0, tm, body, 0, unroll=8)

    def drain(make_copy):
        def body(r, carry):
            make_copy(r, 0).wait()
            return carry

        lax.fori_loop(0, tm, body, 0, unroll=8)

    def step(with_scatter):
        drain(gather_copy)
        xb_s[...] = xbuf[...].astype(BF16)
        issue_inline(gather_copy, i + 1)
        if with_scatter:
            issue_inline(scatter_copy, i - 1)
        _swiglu_hidden(xb_s[...], wg_ref.at[0], wu_ref.at[0], a_s)
        if with_scatter:
            drain(scatter_copy)
        obuf[...] = _dot(a_s[...], wd_ref[0])

    @pl.when(i == 0)
    def _():
        issue_loop(gather_copy, 0)
        step(False)

    @pl.when((i > 0) & (i < n_used))
    def _():
        step(True)

    @pl.when(i == n_used)
    def _():
        drain(gather_copy)
        issue_loop(scatter_copy, i - 1)
        drain(scatter_copy)


def _ffn_moe(h2d, tile_expert, n_used, code, n_y_rows, wg, wu, wd):
    d_ff = wg.shape[2]
    tm = MOE_TM
    n_steps = tile_expert.shape[0]
    assert code.shape[0] == n_steps * tm
    resident = dict(pipeline_mode=pl.Buffered(1))
    w_map = lambda i, te, nu, code: (te[i], 0, 0)
    return pl.pallas_call(
        _ffn_moe_kernel,
        grid_spec=pltpu.PrefetchScalarGridSpec(
            num_scalar_prefetch=3,
            grid=(n_steps,),
            in_specs=[
                pl.BlockSpec(memory_space=pl.ANY),
                pl.BlockSpec((1, D_MODEL, d_ff), w_map, **resident),
                pl.BlockSpec((1, D_MODEL, d_ff), w_map, **resident),
                pl.BlockSpec((1, d_ff, D_MODEL), w_map, **resident),
            ],
            out_specs=pl.BlockSpec(memory_space=pl.ANY),
            scratch_shapes=[
                pltpu.VMEM((tm, D_MODEL), F32),
                pltpu.VMEM((tm, D_MODEL), BF16),
                pltpu.VMEM((tm, d_ff), BF16),
                pltpu.VMEM((tm, D_MODEL), F32),
                pltpu.SemaphoreType.DMA,
                pltpu.SemaphoreType.DMA,
            ],
        ),
        out_shape=jax.ShapeDtypeStruct((n_y_rows, D_MODEL), F32),
        compiler_params=pltpu.CompilerParams(
            dimension_semantics=("arbitrary",), vmem_limit_bytes=VMEM_LIMIT_BYTES),
        name="ffn_moe",
    )(tile_expert, n_used, code, h2d, wg.astype(BF16), wu.astype(BF16), wd.astype(BF16))


COMBINE_TS = 512


def _combine_kernel(h_ref, wts_ref, y1_ref, y2_ref, lg_ref, lb_ref, o_ref):
    w = wts_ref[...]
    f = w[:, 0:1] * y1_ref[...] + w[:, 1:2] * y2_ref[...]
    o_ref[...] = _layer_norm_rows(ALPHA * h_ref[...] + f, lg_ref[...], lb_ref[...])


def _combine(h2d, wts, y, ln_g, ln_b):
    n_tok = h2d.shape[0]
    ts = COMBINE_TS
    assert n_tok % ts == 0
    row = lambda a: a.reshape(1, -1)
    second = n_tok // ts
    return pl.pallas_call(
        _combine_kernel,
        grid=(n_tok // ts,),
        in_specs=[
            pl.BlockSpec((ts, D_MODEL), lambda i: (i, 0)),
            pl.BlockSpec((ts, ROUTE_COLS), lambda i: (i, 0)),
            pl.BlockSpec((ts, D_MODEL), lambda i: (i, 0)),
            pl.BlockSpec((ts, D_MODEL), lambda i: (second + i, 0)),
            pl.BlockSpec((1, D_MODEL), lambda i: (0, 0)),
            pl.BlockSpec((1, D_MODEL), lambda i: (0, 0)),
        ],
        out_specs=pl.BlockSpec((ts, D_MODEL), lambda i: (i, 0)),
        out_shape=jax.ShapeDtypeStruct((n_tok, D_MODEL), F32),
        compiler_params=pltpu.CompilerParams(
            dimension_semantics=("arbitrary",), vmem_limit_bytes=VMEM_LIMIT_BYTES),
        name="moe_combine",
    )(h2d, wts, y, y, row(ln_g), row(ln_b))


def _moe_layer(h2d, w_router, b_router, wg, wu, wd, ln_g, ln_b):
    n_tok = h2d.shape[0]
    tm = MOE_TM
    idx, wts, cnt = _route(h2d, w_router, b_router)

    counts = cnt[0, :N_EXPERTS].astype(jnp.int32)
    padded = ((counts + tm - 1) // tm) * tm
    ends = jnp.cumsum(padded)
    starts = ends - padded
    n_steps = -(-((TOP_K * n_tok) // tm + N_EXPERTS + 1) // INVERT_STEPS) * INVERT_STEPS
    n_used = (ends[-1] // tm).reshape(1)
    tile_id = jnp.minimum(jnp.arange(n_steps, dtype=jnp.int32), n_used - 1)
    tile_expert = jnp.minimum(
        jnp.sum((ends[None, :] <= (tile_id * tm)[:, None]).astype(jnp.int32), axis=1), N_EXPERTS - 1)
    onehot = lambda e: (e[:, None] == jnp.arange(N_EXPERTS, dtype=jnp.int32)[None, :]).astype(jnp.int32)
    pos1 = jnp.sum(onehot(idx[:, 0]) * starts[None, :], axis=1) + idx[:, 2]
    pos2 = jnp.sum(onehot(idx[:, 1]) * starts[None, :], axis=1) + idx[:, 3]

    n_code = n_steps * tm
    code = _invert(pos1, pos2, n_code)
    y = _ffn_moe(h2d, tile_expert, n_used, code, TOP_K * n_tok + n_code, wg, wu, wd)
    return _combine(h2d, wts, y, ln_g, ln_b)


def kernel(x, w_in, b_gates, conv_w, conv_b, pool_w, pool_scale, mlstm_g, w_out, ln1_g, ln1_b, ln2_g, ln2_b, wg_dense, wu_dense, wd_dense, w_router, b_router, wg_exp, wu_exp, wd_exp):
    bsz, seq, d = x.shape
    x2d = x.reshape(bsz * seq, d)
    for l in range(DEPTH):
        h2d = _mixer_layer(x2d, bsz, seq, w_in[l], b_gates[l], conv_w[l], conv_b[l], pool_w[l],
                           pool_scale[l], mlstm_g[l], w_out[l], ln1_g[l], ln1_b[l])
        j = l // 2
        if l % 2 == 0:
            x2d = _ffn_dense_layer(h2d, wg_dense[j], wu_dense[j], wd_dense[j], ln2_g[l], ln2_b[l])
        else:
            x2d = _moe_layer(h2d, w_router[j], b_router[j], wg_exp[j], wu_exp[j], wd_exp[j],
                             ln2_g[l], ln2_b[l])
    return x2d.reshape(bsz, seq, d)
```

```python
import functools

import jax
import jax.numpy as jnp
from jax import lax
from jax.experimental import pallas as pl
from jax.experimental.pallas import tpu as pltpu

F32 = jnp.float32
BF16 = jnp.bfloat16

D_MODEL = 1024
DEPTH = 4
D_POOL = 512
POOL_WINDOWS = (2, 4, 8, 16)
POOL_GROUP = 128
MAX_WINDOW = max(POOL_WINDOWS)
D_MLSTM = 512
N_HEADS = 4
HEAD_DIM = 128
CONV_WIDTH = 4
N_GATES = 2 * N_HEADS
D_IN = D_POOL + 4 * D_MLSTM + N_GATES
N_EXPERTS = 8
TOP_K = 2
ALPHA = (2 * DEPTH) ** 0.25
LN_EPS = 1e-5

LANES = 128
SUBLANES = 8
D_IN_PAD = D_POOL + 4 * D_MLSTM + LANES
VMEM_LIMIT_BYTES = 56 * 1024 * 1024

OFF_QK = D_POOL
OFF_V = OFF_QK + 2 * D_MLSTM
OFF_O = OFF_V + D_MLSTM
OFF_G = OFF_O + D_MLSTM

MIX_TS = 512
MIX_L = 256
CONV_PAD = SUBLANES
POOL_PAD = 16


def _dot(a, b):
    return jnp.dot(a, b, preferred_element_type=F32)


def _sigmoid(x):
    return 1.0 / (1.0 + jnp.exp(-x))


def _layer_norm_rows(r, g, b):
    mu = jnp.mean(r, axis=-1, keepdims=True)
    rc = r - mu
    var = jnp.mean(rc * rc, axis=-1, keepdims=True)
    return rc * lax.rsqrt(var + LN_EPS) * g + b


def _mixer_kernel(x_ref, win_ref, bg_ref, cw_ref, cb_ref, pw_ref, ps_ref, mg_ref, wout_ref,
                  lg_ref, lb_ref, h_ref,
                  pbuf, cbuf, q_s, k_s, ycat, st_s, m_s):
    ts, L = MIX_TS, MIX_L
    s_idx = pl.program_id(1)

    @pl.when(s_idx == 0)
    def _():
        pbuf[0:POOL_PAD, :] = jnp.zeros((POOL_PAD, D_POOL), F32)
        cbuf[0:CONV_PAD, :] = jnp.zeros((CONV_PAD, 2 * D_MLSTM), F32)
        st_s[...] = jnp.zeros_like(st_s)
        m_s[...] = jnp.zeros_like(m_s)

    xb = x_ref[...].astype(BF16)
    gates =_dot(xb, win_ref[:, OFF_G:D_IN_PAD]) + bg_ref[...]

    g_row = gates.T[0:N_GATES, :]
    logf_row = jnp.minimum(g_row, 0.0) - jnp.log1p(jnp.exp(-jnp.abs(g_row)))
    lane_in_chunk = lax.broadcasted_iota(jnp.int32, (N_GATES, ts), 1) % L
    b_row = logf_row
    sh = 1
    while sh < L:
        b_row = b_row + jnp.where(lane_in_chunk >= sh, pltpu.roll(b_row, sh, axis=1), 0.0)
        sh *= 2
    b_heads = b_row[N_HEADS:N_GATES, :]
    a_row = g_row[0:N_HEADS, :] - b_heads
    cmax_row = a_row
    sh = 1
    while sh < L:
        cmax_row = jnp.maximum(cmax_row, jnp.where(lane_in_chunk[0:N_HEADS] >= sh,
                                                   pltpu.roll(cmax_row, sh, axis=1), -jnp.inf))
        sh *= 2
    cols = jnp.concatenate([cmax_row, b_heads, jnp.zeros((LANES - N_GATES, ts), F32)], axis=0).T

    cbuf[CONV_PAD:CONV_PAD + ts, :] = _dot(xb, win_ref[:, OFF_QK:OFF_V])
    pbuf[POOL_PAD:POOL_PAD + ts, :] = _dot(xb, win_ref[:, 0:OFF_QK])
    v = _dot(xb, win_ref[:, OFF_V:OFF_O]).astype(BF16)
    o_pre = _dot(xb, win_ref[:, OFF_O:OFF_G])

    for c in range(2 * D_MLSTM // LANES):
        cs = slice(c * LANES, (c + 1) * LANES)
        y = cb_ref[:, cs] + cbuf[CONV_PAD:CONV_PAD + ts, cs] * cw_ref[CONV_WIDTH - 1:CONV_WIDTH, cs]
        for j in range(1, CONV_WIDTH):
            y = y + cbuf[CONV_PAD - j:CONV_PAD - j + ts, cs] * cw_ref[CONV_WIDTH - 1 - j:CONV_WIDTH - j, cs]
        y = y * _sigmoid(y)
        if c < N_HEADS:
            q_s[:, cs] = y.astype(BF16)
        else:
            ks = slice((c - N_HEADS) * LANES, (c - N_HEADS + 1) * LANES)
            k_s[:, ks] = (y * (HEAD_DIM ** -0.5)).astype(BF16)
    cbuf[0:CONV_PAD, :] = cbuf[ts:ts + CONV_PAD, :]

    t_glob = lax.broadcasted_iota(jnp.int32, (ts, 1), 0) + s_idx * ts
    for g, w in enumerate(POOL_WINDOWS):
        cs = slice(g * POOL_GROUP, (g + 1) * POOL_GROUP)
        acc = pbuf[:, cs]
        span = 1
        while span < w:
            acc = acc + pltpu.roll(acc, span, axis=0)
            span *= 2
        acc = acc[POOL_PAD:POOL_PAD + ts, :]
        u = pbuf[POOL_PAD:POOL_PAD + ts, cs]
        cnt = jnp.minimum(t_glob + 1, w).astype(F32)
        d = acc / cnt - u
        y = _dot(d.astype(BF16), pw_ref[g]) * ps_ref[:, cs]
        ycat[:, cs] = y.astype(BF16)
    pbuf[0:POOL_PAD, :] = pbuf[ts:ts + POOL_PAD, :]

    tri = (lax.broadcasted_iota(jnp.int32, (L, L), 1) <= lax.broadcasted_iota(jnp.int32, (L, L), 0))
    dn_nt = (((1,), (1,)), ((), ()))
    dn_tn = (((0,), (0,)), ((), ()))
    ones_blk = jnp.ones((L, HEAD_DIM), BF16)

    state = [st_s[h] for h in range(N_HEADS)]
    m_state = [m_s[h:h + 1, 0:1] for h in range(N_HEADS)]
    for c in range(ts // L):
        rs = slice(c * L, (c + 1) * L)
        for h in range(N_HEADS):
            hs = slice(h * HEAD_DIM, (h + 1) * HEAD_DIM)
            cm_c = cols[rs, h:h + 1]
            bc = cols[rs, N_HEADS + h:N_HEADS + h + 1]
            ic = gates[rs, h:h + 1]
            ar = a_row[h:h + 1, rs]
            m_prev = m_state[h]
            q_h = q_s[rs, hs]
            k_h = k_s[rs, hs]
            v2 = jnp.concatenate([v[rs, hs], ones_blk], axis=1)

            mm = jnp.maximum(cm_c, m_prev)
            s_qk = lax.dot_general(q_h, k_h, dn_nt, preferred_element_type=F32)
            wmat = jnp.exp(jnp.where(tri, ar - mm, -jnp.inf)) * s_qk
            s_inter = jnp.exp(m_prev - mm)
            tot = _dot(wmat.astype(BF16), v2) + s_inter * _dot(q_h, state[h].astype(BF16))
            num = tot[:, 0:HEAD_DIM]
            den = tot[:, HEAD_DIM:2 * HEAD_DIM]
            hh = num / jnp.maximum(jnp.abs(den), jnp.exp(-(bc + mm)))

            mu = jnp.mean(hh, axis=1, keepdims=True)
            hc = hh - mu
            var = jnp.mean(hc * hc, axis=1, keepdims=True)
            hn = hc * lax.rsqrt(var + LN_EPS) * mg_ref[:, hs]
            ym = _sigmoid(o_pre[rs, hs]) * hn
            ycat[rs, D_POOL + h * HEAD_DIM:D_POOL + (h + 1) * HEAD_DIM] = ym.astype(BF16)

            g_tot = bc[L - 1:L, :]
            m_loc = g_tot + cm_c[L - 1:L, :]
            wa = jnp.exp(g_tot + ic - bc - m_loc)
            wv2 = (wa * v2.astype(F32)).astype(BF16)
            st_loc = lax.dot_general(k_h, wv2, dn_tn, preferred_element_type=F32)
            m_new = jnp.maximum(g_tot + m_prev, m_loc)
            state[h] = jnp.exp(g_tot + m_prev - m_new) * state[h] + jnp.exp(m_loc - m_new) * st_loc
            m_state[h] = m_new
        mix = _dot(ycat[rs, :], wout_ref[...])
        h_ref[rs, :] = _layer_norm_rows(ALPHA * x_ref[rs, :] + mix, lg_ref[...], lb_ref[...])
    for h in range(N_HEADS):
        st_s[h] = state[h]
        m_s[h:h + 1, :] = jnp.broadcast_to(m_state[h], (1, LANES))


def _mixer_layer(x2d, bsz, seq, w_in, b_gates, conv_w, conv_b, pool_w, pool_scale, mlstm_g, w_out,
                 ln_g, ln_b):
    ts = MIX_TS
    assert seq % ts == 0 and ts % MIX_L == 0
    n_s_tiles = seq // ts
    w_in_p = jnp.pad(w_in, ((0, 0), (0, D_IN_PAD - D_IN))).astype(BF16)
    bg_p = jnp.pad(b_gates, (0, LANES - N_GATES)).reshape(1, LANES)
    row = lambda a: a.reshape(1, -1)
    const2 = lambda b, s: (0, 0)
    const3 = lambda b, s: (0, 0, 0)
    return pl.pallas_call(
        _mixer_kernel,
        grid=(bsz, n_s_tiles),
        in_specs=[
            pl.BlockSpec((ts, D_MODEL), lambda b, s: (b * n_s_tiles + s, 0)),
            pl.BlockSpec((D_MODEL, D_IN_PAD), const2),
            pl.BlockSpec((1, LANES), const2),
            pl.BlockSpec((CONV_WIDTH, 2 * D_MLSTM), const2),
            pl.BlockSpec((1, 2 * D_MLSTM), const2),
            pl.BlockSpec((len(POOL_WINDOWS), POOL_GROUP, POOL_GROUP), const3),
            pl.BlockSpec((1, D_POOL), const2),
            pl.BlockSpec((1, D_MLSTM), const2),
            pl.BlockSpec((D_MODEL, D_MODEL), const2),
            pl.BlockSpec((1, D_MODEL), const2),
            pl.BlockSpec((1, D_MODEL), const2),
        ],
        out_specs=pl.BlockSpec((ts, D_MODEL), lambda b, s: (b * n_s_tiles + s, 0)),
        out_shape=jax.ShapeDtypeStruct((bsz * seq, D_MODEL), F32),
        scratch_shapes=[
            pltpu.VMEM((POOL_PAD + ts, D_POOL), F32),
            pltpu.VMEM((CONV_PAD + ts, 2 * D_MLSTM), F32),
            pltpu.VMEM((ts, D_MLSTM), BF16),
            pltpu.VMEM((ts, D_MLSTM), BF16),
            pltpu.VMEM((ts, D_MODEL), BF16),
            pltpu.VMEM((N_HEADS, HEAD_DIM, 2 * HEAD_DIM), F32),
            pltpu.VMEM((SUBLANES, LANES), F32),
        ],
        compiler_params=pltpu.CompilerParams(
            dimension_semantics=("arbitrary", "arbitrary"),
            vmem_limit_bytes=VMEM_LIMIT_BYTES),
        name="mixer",
    )(x2d, w_in_p, bg_p, conv_w, row(conv_b), pool_w.astype(BF16), row(pool_scale), row(mlstm_g),
      w_out.astype(BF16), row(ln_g), row(ln_b))


DENSE_TM = 512
FF_CHUNK = 512


def _swiglu_hidden(xb, wg, wu, a_s, before_chunk=None):
    d_ff = a_s.shape[1]
    for c, c0 in enumerate(range(0, d_ff, FF_CHUNK)):
        if before_chunk is not None:
            before_chunk(c)
        cs = slice(c0, min(c0 + FF_CHUNK, d_ff))
        g = _dot(xb, wg[:, cs])
        u = _dot(xb, wu[:, cs])
        a_s[:, cs] = (g * _sigmoid(g) * u).astype(BF16)


def _ffn_dense_kernel(x_ref, wg_ref, wu_ref, wd_ref, lg_ref, lb_ref, o_ref, a_s):
    x = x_ref[...]
    _swiglu_hidden(x.astype(BF16), wg_ref, wu_ref, a_s)
    f = _dot(a_s[...], wd_ref[...])
    o_ref[...] = _layer_norm_rows(ALPHA * x + f, lg_ref[...], lb_ref[...])


def _ffn_dense_layer(h2d, wg, wu, wd, ln_g, ln_b):
    n_tok = h2d.shape[0]
    d_ff = wg.shape[1]
    tm = DENSE_TM
    assert n_tok % tm == 0
    row = lambda a: a.reshape(1, -1)
    resident = dict(pipeline_mode=pl.Buffered(1))
    return pl.pallas_call(
        _ffn_dense_kernel,
        grid=(n_tok // tm,),
        in_specs=[
            pl.BlockSpec((tm, D_MODEL), lambda i: (i, 0)),
            pl.BlockSpec((D_MODEL, d_ff), lambda i: (0, 0), **resident),
            pl.BlockSpec((D_MODEL, d_ff), lambda i: (0, 0), **resident),
            pl.BlockSpec((d_ff, D_MODEL), lambda i: (0, 0), **resident),
            pl.BlockSpec((1, D_MODEL), lambda i: (0, 0)),
            pl.BlockSpec((1, D_MODEL), lambda i: (0, 0)),
        ],
        out_specs=pl.BlockSpec((tm, D_MODEL), lambda i: (i, 0)),
        out_shape=jax.ShapeDtypeStruct((n_tok, D_MODEL), F32),
        scratch_shapes=[pltpu.VMEM((tm, d_ff), BF16)],
        compiler_params=pltpu.CompilerParams(
            dimension_semantics=("arbitrary",), vmem_limit_bytes=VMEM_LIMIT_BYTES),
        name="ffn_dense",
    )(h2d, wg.astype(BF16), wu.astype(BF16), wd.astype(BF16), row(ln_g), row(ln_b))


ROUTE_TS = 512
MOE_TM = 512
ROUTE_COLS = 8
INVERT_STEPS = 16


def _route_kernel(h_ref, wr_ref, br_ref, idx_ref, wts_ref, cnt_ref, carry_s):
    ts = ROUTE_TS
    i = pl.program_id(0)

    @pl.when(i == 0)
    def _():
        carry_s[...] = jnp.zeros_like(carry_s)

    h = h_ref[...]
    h_hi = h.astype(BF16)
    h_lo = (h - h_hi.astype(F32)).astype(BF16)
    logits = (_dot(h_hi, wr_ref[0]) + _dot(h_lo, wr_ref[0]) + _dot(h_hi, wr_ref[1])) + br_ref[...]
    lane = lax.broadcasted_iota(jnp.int32, (ts, LANES), 1)
    lg = jnp.where(lane < N_EXPERTS, logits, -jnp.inf)
    m1 = jnp.max(lg, axis=1, keepdims=True)
    i1 = jnp.min(jnp.where(lg == m1, lane, LANES), axis=1, keepdims=True)
    oh1 = lane == i1
    lg2 = jnp.where(oh1, -jnp.inf, lg)
    m2 = jnp.max(lg2, axis=1, keepdims=True)
    i2 = jnp.min(jnp.where(lg2 == m2, lane, LANES), axis=1, keepdims=True)
    oh2 = lane == i2
    e2 = jnp.exp(m2 - m1)
    w1 = 1.0 / (1.0 + e2)
    w2 = e2 / (1.0 + e2)

    sel = jnp.where(oh1 | oh2, 1.0, 0.0)
    strict_lower = jnp.where(lax.broadcasted_iota(jnp.int32, (ts, ts), 1)
                             < lax.broadcasted_iota(jnp.int32, (ts, ts), 0), 1.0, 0.0).astype(BF16)
    before = _dot(strict_lower, sel.astype(BF16)) + carry_s[...]
    r1 = jnp.sum(jnp.where(oh1, before, 0.0), axis=1, keepdims=True).astype(jnp.int32)
    r2 = jnp.sum(jnp.where(oh2, before, 0.0), axis=1, keepdims=True).astype(jnp.int32)
    carry_s[...] += jnp.sum(sel, axis=0, keepdims=True)

    col = lax.broadcasted_iota(jnp.int32, (ts, ROUTE_COLS), 1)
    idx_ref[...] = jnp.where(col == 0, i1, jnp.where(col == 1, i2, jnp.where(col == 2, r1, r2)))
    wts_ref[...] = jnp.where(col == 0, w1, w2)
    cnt_ref[...] = carry_s[...]


def _route(h2d, w_router, b_router):
    n_tok = h2d.shape[0]
    ts = ROUTE_TS
    assert n_tok % ts == 0
    wr = jnp.pad(w_router, ((0, 0), (0, LANES - N_EXPERTS)))
    wr_hi = wr.astype(BF16)
    wr = jnp.stack([wr_hi, (wr - wr_hi.astype(F32)).astype(BF16)])
    br = jnp.pad(b_router, (0, LANES - N_EXPERTS)).reshape(1, LANES)
    return pl.pallas_call(
        _route_kernel,
        grid=(n_tok // ts,),
        in_specs=[
            pl.BlockSpec((ts, D_MODEL), lambda i: (i, 0)),
            pl.BlockSpec((2, D_MODEL, LANES), lambda i: (0, 0, 0)),
            pl.BlockSpec((1, LANES), lambda i: (0, 0)),
        ],
        out_specs=[
            pl.BlockSpec((ts, ROUTE_COLS), lambda i: (i, 0)),
            pl.BlockSpec((ts, ROUTE_COLS), lambda i: (i, 0)),
            pl.BlockSpec((1, LANES), lambda i: (0, 0)),
        ],
        out_shape=[
            jax.ShapeDtypeStruct((n_tok, ROUTE_COLS), jnp.int32),
            jax.ShapeDtypeStruct((n_tok, ROUTE_COLS), F32),
            jax.ShapeDtypeStruct((1, LANES), F32),
        ],
        scratch_shapes=[pltpu.VMEM((1, LANES), F32)],
        compiler_params=pltpu.CompilerParams(
            dimension_semantics=("arbitrary",), vmem_limit_bytes=VMEM_LIMIT_BYTES),
        name="moe_route",
    )(h2d, wr, br)


def _row_copy(src_ref, src_row, dst_ref, dst_row, sem):
    return pltpu.make_async_copy(src_ref.at[pl.ds(src_row, 1), :], dst_ref.at[pl.ds(dst_row, 1), :], sem)


def _invert_kernel(pos1_ref, pos2_ref, default_hbm, code_ref, sem):
    n_tok = pos1_ref.shape[0]
    s = pl.program_id(0)
    toks_per_step = n_tok // INVERT_STEPS

    @pl.when(s == 0)
    def _():
        fill = pltpu.make_async_copy(default_hbm, code_ref, sem)
        fill.start()
        fill.wait()

    base = s * toks_per_step

    def scatter(j, carry):
        t = base + j
        code_ref[pos1_ref[t]] = t
        code_ref[pos2_ref[t]] = n_tok + t
        return carry

    lax.fori_loop(0, toks_per_step, scatter, 0, unroll=8)


def _invert(pos1, pos2, n_code):
    n_tok = pos1.shape[0]
    assert n_tok % INVERT_STEPS == 0
    smem = pl.BlockSpec(memory_space=pltpu.SMEM)
    default = 2 * n_tok + jnp.arange(n_code, dtype=jnp.int32)
    return pl.pallas_call(
        _invert_kernel,
        grid=(INVERT_STEPS,),
        in_specs=[smem, smem, pl.BlockSpec(memory_space=pl.ANY)],
        out_specs=smem,
        out_shape=jax.ShapeDtypeStruct((n_code,), jnp.int32),
        scratch_shapes=[pltpu.SemaphoreType.DMA],
        compiler_params=pltpu.CompilerParams(dimension_semantics=("arbitrary",)),
        name="moe_invert",
    )(pos1, pos2, default)


def _ffn_moe_kernel(layer, te_ref, nu_ref, code_ref, h_hbm, wg_hbm, wu_hbm, wd_hbm, y_hbm,
                    xbuf, xb_s, a_s, obuf, wg_s, wu_s, wd_s, stage_in, stage_out, gsem, ssem, wsem):
    tm = MOE_TM
    n_tok = h_hbm.shape[0]
    d_ff = a_s.shape[1]
    n_chunks = d_ff // FF_CHUNK
    i = pl.program_id(0)
    n_used = nu_ref[0]

    def stream_cast(src_chunk, store_chunk, stage):
        copy = lambda c: pltpu.make_async_copy(src_chunk(c), stage.at[c % 2], wsem.at[c % 2])
        copy(0).start()
        for c in range(n_chunks):
            if c + 1 < n_chunks:
                copy(c + 1).start()
            copy(c).wait()
            store_chunk(c, stage[c % 2].astype(BF16))

    def load_expert(e):
        col = lambda c: pl.ds(c * FF_CHUNK, FF_CHUNK)

        def store_cols(dst):
            def store(c, val):
                dst[:, c * FF_CHUNK:(c + 1) * FF_CHUNK] = val
            return store

        def store_rows(c, val):
            wd_s[c * FF_CHUNK:(c + 1) * FF_CHUNK, :] = val

        stream_cast(lambda c: wg_hbm.at[layer, e, :, col(c)], store_cols(wg_s), stage_in)
        stream_cast(lambda c: wu_hbm.at[layer, e, :, col(c)], store_cols(wu_s), stage_in)
        stream_cast(lambda c: wd_hbm.at[layer, e, col(c), :], store_rows, stage_out)

    @pl.when((i < n_used) & ((i == 0) | (te_ref[i] != te_ref[jnp.maximum(i - 1, 0)])))
    def _():
        load_expert(te_ref[i])

    def gather_copy(r, code):
        tok = code & (n_tok - 1) if n_tok & (n_tok - 1) == 0 else lax.rem(code, n_tok)
        return _row_copy(h_hbm, tok, xbuf, r, gsem)

    def scatter_copy(r, code):
        return _row_copy(obuf, r, y_hbm, code, ssem)

    def issue_inline(make_copy, tile, part):
        for r in range(part * tm // n_chunks, (part + 1) * tm // n_chunks):
            make_copy(r, code_ref[tile * tm + r]).start(priority=r % 2)

    def issue_loop(make_copy, tile):
        def body(r, carry):
            make_copy(r, code_ref[tile * tm + r]).start()
            return carry

        lax.fori_loop(0, tm, body, 0, unroll=8)

    def drain(make_copy):
        def body(r, carry):
            make_copy(r, 0).wait()
            return carry

        lax.fori_loop(0, tm, body, 0, unroll=8)

    def step(with_scatter):
        drain(gather_copy)
        xb_s[...] = xbuf[...].astype(BF16)

        def before_chunk(c):
            issue_inline(gather_copy, i + 1, c)
            if with_scatter:
                issue_inline(scatter_copy, i - 1, c)

        _swiglu_hidden(xb_s[...], wg_s, wu_s, a_s, before_chunk)
        if with_scatter:
            drain(scatter_copy)
        obuf[...] = _dot(a_s[...], wd_s[...])

    @pl.when(i == 0)
    def _():
        issue_loop(gather_copy, 0)
        step(False)

    @pl.when((i > 0) & (i < n_used))
    def _():
        step(True)

    @pl.when(i == n_used)
    def _():
        drain(gather_copy)
        issue_loop(scatter_copy, i - 1)
        drain(scatter_copy)


def _ffn_moe(h2d, tile_expert, n_used, code, n_y_rows, layer, wg_all, wu_all, wd_all):
    d_ff = wg_all.shape[3]
    tm = MOE_TM
    n_steps = tile_expert.shape[0]
    assert code.shape[0] == n_steps * tm and d_ff % FF_CHUNK == 0
    hbm = pl.BlockSpec(memory_space=pl.ANY)
    return pl.pallas_call(
        functools.partial(_ffn_moe_kernel, layer),
        grid_spec=pltpu.PrefetchScalarGridSpec(
            num_scalar_prefetch=3,
            grid=(n_steps,),
            in_specs=[hbm, hbm, hbm, hbm],
            out_specs=hbm,
            scratch_shapes=[
                pltpu.VMEM((tm, D_MODEL), F32),
                pltpu.VMEM((tm, D_MODEL), BF16),
                pltpu.VMEM((tm, d_ff), BF16),
                pltpu.VMEM((tm, D_MODEL), F32),
                pltpu.VMEM((D_MODEL, d_ff), BF16),
                pltpu.VMEM((D_MODEL, d_ff), BF16),
                pltpu.VMEM((d_ff, D_MODEL), BF16),
                pltpu.VMEM((2, D_MODEL, FF_CHUNK), F32),
                pltpu.VMEM((2, FF_CHUNK, D_MODEL), F32),
                pltpu.SemaphoreType.DMA,
                pltpu.SemaphoreType.DMA,
                pltpu.SemaphoreType.DMA((2,)),
            ],
        ),
        out_shape=jax.ShapeDtypeStruct((n_y_rows, D_MODEL), F32),
        compiler_params=pltpu.CompilerParams(
            dimension_semantics=("arbitrary",), vmem_limit_bytes=VMEM_LIMIT_BYTES),
        name="ffn_moe",
    )(tile_expert, n_used, code, h2d, wg_all, wu_all, wd_all)


COMBINE_TS = 512


def _combine_kernel(h_ref, wts_ref, y1_ref, y2_ref, lg_ref, lb_ref, o_ref):
    w = wts_ref[...]
    f = w[:, 0:1] * y1_ref[...] + w[:, 1:2] * y2_ref[...]
    o_ref[...] = _layer_norm_rows(ALPHA * h_ref[...] + f, lg_ref[...], lb_ref[...])


def _combine(h2d, wts, y, ln_g, ln_b):
    n_tok = h2d.shape[0]
    ts = COMBINE_TS
    assert n_tok % ts == 0
    row = lambda a: a.reshape(1, -1)
    second = n_tok // ts
    return pl.pallas_call(
        _combine_kernel,
        grid=(n_tok // ts,),
        in_specs=[
            pl.BlockSpec((ts, D_MODEL), lambda i: (i, 0)),
            pl.BlockSpec((ts, ROUTE_COLS), lambda i: (i, 0)),
            pl.BlockSpec((ts, D_MODEL), lambda i: (i, 0)),
            pl.BlockSpec((ts, D_MODEL), lambda i: (second + i, 0)),
            pl.BlockSpec((1, D_MODEL), lambda i: (0, 0)),
            pl.BlockSpec((1, D_MODEL), lambda i: (0, 0)),
        ],
        out_specs=pl.BlockSpec((ts, D_MODEL), lambda i: (i, 0)),
        out_shape=jax.ShapeDtypeStruct((n_tok, D_MODEL), F32),
        compiler_params=pltpu.CompilerParams(
            dimension_semantics=("arbitrary",), vmem_limit_bytes=VMEM_LIMIT_BYTES),
        name="moe_combine",
    )(h2d, wts, y, y, row(ln_g), row(ln_b))


def _moe_layer(h2d, w_router, b_router, layer, wg_all, wu_all, wd_all, ln_g, ln_b):
    n_tok = h2d.shape[0]
    tm = MOE_TM
    idx, wts, cnt = _route(h2d, w_router, b_router)

    counts = cnt[0, :N_EXPERTS].astype(jnp.int32)
    padded = ((counts + tm - 1) // tm) * tm
    ends = jnp.cumsum(padded)
    starts = ends - padded
    n_steps = (TOP_K * n_tok) // tm + N_EXPERTS + 1
    n_steps += n_steps % 2
    n_used = (ends[-1] // tm).reshape(1)
    tile_id = jnp.minimum(jnp.arange(n_steps, dtype=jnp.int32), n_used - 1)
    tile_expert = jnp.minimum(
        jnp.sum((ends[None, :] <= (tile_id * tm)[:, None]).astype(jnp.int32), axis=1), N_EXPERTS - 1)
    onehot = lambda e: (e[:, None] == jnp.arange(N_EXPERTS, dtype=jnp.int32)[None, :]).astype(jnp.int32)
    pos1 = jnp.sum(onehot(idx[:, 0]) * starts[None, :], axis=1) + idx[:, 2]
    pos2 = jnp.sum(onehot(idx[:, 1]) * starts[None, :], axis=1) + idx[:, 3]

    n_code = n_steps * tm
    code = _invert(pos1, pos2, n_code)
    y = _ffn_moe(h2d, tile_expert, n_used, code, TOP_K * n_tok + n_code, layer, wg_all, wu_all, wd_all)
    return _combine(h2d, wts, y, ln_g, ln_b)


def kernel(x, w_in, b_gates, conv_w, conv_b, pool_w, pool_scale, mlstm_g, w_out, ln1_g, ln1_b, ln2_g, ln2_b, wg_dense, wu_dense, wd_dense, w_router, b_router, wg_exp, wu_exp, wd_exp):
    bsz, seq, d = x.shape
    x2d = x.reshape(bsz * seq, d)
    for l in range(DEPTH):
        h2d = _mixer_layer(x2d, bsz, seq, w_in[l], b_gates[l], conv_w[l], conv_b[l], pool_w[l],
                           pool_scale[l], mlstm_g[l], w_out[l], ln1_g[l], ln1_b[l])
        j = l // 2
        if l % 2 == 0:
            x2d = _ffn_dense_layer(h2d, wg_dense[j], wu_dense[j], wd_dense[j], ln2_g[l], ln2_b[l])
        else:
            x2d = _moe_layer(h2d, w_router[j], b_router[j], j, wg_exp, wu_exp, wd_exp,
                             ln2_g[l], ln2_b[l])
    return x2d.reshape(bsz, seq, d)
```

```python
import functools

import jax
import jax.numpy as jnp
from jax import lax
from jax.experimental import pallas as pl
from jax.experimental.pallas import tpu as pltpu

F32 = jnp.float32
BF16 = jnp.bfloat16

D_MODEL = 1024
DEPTH = 4
D_POOL = 512
POOL_WINDOWS = (2, 4, 8, 16)
POOL_GROUP = 128
MAX_WINDOW = max(POOL_WINDOWS)
D_MLSTM = 512
N_HEADS = 4
HEAD_DIM = 128
CONV_WIDTH = 4
N_GATES = 2 * N_HEADS
D_IN = D_POOL + 4 * D_MLSTM + N_GATES
N_EXPERTS = 8
TOP_K = 2
ALPHA = (2 * DEPTH) ** 0.25
LN_EPS = 1e-5

LANES = 128
SUBLANES = 8
D_IN_PAD = D_POOL + 4 * D_MLSTM + LANES
VMEM_LIMIT_BYTES = 56 * 1024 * 1024

OFF_QK = D_POOL
OFF_V = OFF_QK + 2 * D_MLSTM
OFF_O = OFF_V + D_MLSTM
OFF_G = OFF_O + D_MLSTM

MIX_TS = 512
MIX_L = 256
CONV_PAD = SUBLANES
POOL_PAD = 16


def _dot(a, b):
    return jnp.dot(a, b, preferred_element_type=F32)


def _sigmoid(x):
    return 1.0 / (1.0 + jnp.exp(-x))


def _layer_norm_rows(r, g, b):
    mu = jnp.mean(r, axis=-1, keepdims=True)
    rc = r - mu
    var = jnp.mean(rc * rc, axis=-1, keepdims=True)
    return rc * lax.rsqrt(var + LN_EPS) * g + b


def _mixer_kernel(x_ref, win_ref, bg_ref, cw_ref, cb_ref, pw_ref, ps_ref, mg_ref, wout_ref,
                  lg_ref, lb_ref, h_ref,
                  pbuf, cbuf, q_s, k_s, ycat, st_s, m_s):
    ts, L = MIX_TS, MIX_L
    s_idx = pl.program_id(1)

    @pl.when(s_idx == 0)
    def _():
        pbuf[0:POOL_PAD, :] = jnp.zeros((POOL_PAD, D_POOL), F32)
        cbuf[0:CONV_PAD, :] = jnp.zeros((CONV_PAD, 2 * D_MLSTM), F32)
        st_s[...] = jnp.zeros_like(st_s)
        m_s[...] = jnp.zeros_like(m_s)

    xb = x_ref[...].astype(BF16)
    gates = _dot(xb, win_ref[:, OFF_G:D_IN_PAD]) + bg_ref[...]

    g_row = gates.T[0:N_GATES, :]
    logf_row = jnp.minimum(g_row, 0.0) - jnp.log1p(jnp.exp(-jnp.abs(g_row)))
    lane_in_chunk = lax.broadcasted_iota(jnp.int32, (N_GATES, ts), 1) % L
    b_row = logf_row
    sh = 1
    while sh < L:
        b_row = b_row + jnp.where(lane_in_chunk >= sh, pltpu.roll(b_row, sh, axis=1), 0.0)
        sh *= 2
    b_heads = b_row[N_HEADS:N_GATES, :]
    a_row = g_row[0:N_HEADS, :] - b_heads
    cmax_row = a_row
    sh = 1
    while sh < L:
        cmax_row = jnp.maximum(cmax_row, jnp.where(lane_in_chunk[0:N_HEADS] >= sh,
                                                   pltpu.roll(cmax_row, sh, axis=1), -jnp.inf))
        sh *= 2
    cols = jnp.concatenate([cmax_row, b_heads, jnp.zeros((LANES - N_GATES, ts), F32)], axis=0).T

    cbuf[CONV_PAD:CONV_PAD + ts, :] = _dot(xb, win_ref[:, OFF_QK:OFF_V])
    pbuf[POOL_PAD:POOL_PAD + ts, :] = _dot(xb, win_ref[:, 0:OFF_QK])
    v = _dot(xb, win_ref[:, OFF_V:OFF_O]).astype(BF16)
    o_pre = _dot(xb, win_ref[:, OFF_O:OFF_G])

    for c in range(2 * D_MLSTM // LANES):
        cs = slice(c * LANES, (c + 1) * LANES)
        y = cb_ref[:, cs] + cbuf[CONV_PAD:CONV_PAD + ts, cs] * cw_ref[CONV_WIDTH - 1:CONV_WIDTH, cs]
        for j in range(1, CONV_WIDTH):
            y = y + cbuf[CONV_PAD - j:CONV_PAD - j + ts, cs] * cw_ref[CONV_WIDTH - 1 - j:CONV_WIDTH - j, cs]
        y = y * _sigmoid(y)
        if c < N_HEADS:
            q_s[:, cs] = y.astype(BF16)
        else:
            ks = slice((c - N_HEADS) * LANES, (c - N_HEADS + 1) * LANES)
            k_s[:, ks] = (y * (HEAD_DIM ** -0.5)).astype(BF16)
    cbuf[0:CONV_PAD, :] = cbuf[ts:ts + CONV_PAD, :]

    t_glob = lax.broadcasted_iota(jnp.int32, (ts, 1), 0) + s_idx * ts
    for g, w in enumerate(POOL_WINDOWS):
        cs = slice(g * POOL_GROUP, (g + 1) * POOL_GROUP)
        acc = pbuf[:, cs]
        span = 1
        while span < w:
            acc = acc + pltpu.roll(acc, span, axis=0)
            span *= 2
        acc = acc[POOL_PAD:POOL_PAD + ts, :]
        u = pbuf[POOL_PAD:POOL_PAD + ts, cs]
        cnt = jnp.minimum(t_glob + 1, w).astype(F32)
        d = acc / cnt - u
        y = _dot(d.astype(BF16), pw_ref[g]) * ps_ref[:, cs]
        ycat[:, cs] = y.astype(BF16)
    pbuf[0:POOL_PAD, :] = pbuf[ts:ts + POOL_PAD, :]

    tri = (lax.broadcasted_iota(jnp.int32, (L, L), 1) <= lax.broadcasted_iota(jnp.int32, (L, L), 0))
    dn_nt = (((1,), (1,)), ((), ()))
    dn_tn = (((0,), (0,)), ((), ()))
    ones_blk = jnp.ones((L, HEAD_DIM), BF16)

    state = [st_s[h] for h in range(N_HEADS)]
    m_state = [m_s[h:h + 1, 0:1] for h in range(N_HEADS)]
    for c in range(ts // L):
        rs = slice(c * L, (c + 1) * L)
        for h in range(N_HEADS):
            hs = slice(h * HEAD_DIM, (h + 1) * HEAD_DIM)
            cm_c = cols[rs, h:h + 1]
            bc = cols[rs, N_HEADS + h:N_HEADS + h + 1]
            ic = gates[rs, h:h + 1]
            ar = a_row[h:h + 1, rs]
            m_prev = m_state[h]
            q_h = q_s[rs, hs]
            k_h = k_s[rs, hs]
            v2 = jnp.concatenate([v[rs, hs], ones_blk], axis=1)

            mm = jnp.maximum(cm_c, m_prev)
            s_qk = lax.dot_general(q_h, k_h, dn_nt, preferred_element_type=F32)
            wmat = jnp.exp(jnp.where(tri, ar - mm, -jnp.inf)) * s_qk
            s_inter = jnp.exp(m_prev - mm)
            tot = _dot(wmat.astype(BF16), v2) + s_inter * _dot(q_h, state[h].astype(BF16))
            num = tot[:, 0:HEAD_DIM]
            den = tot[:, HEAD_DIM:2 * HEAD_DIM]
            hh = num / jnp.maximum(jnp.abs(den), jnp.exp(-(bc + mm)))

            mu = jnp.mean(hh, axis=1, keepdims=True)
            hc = hh - mu
            var = jnp.mean(hc * hc, axis=1, keepdims=True)
            hn = hc * lax.rsqrt(var + LN_EPS) * mg_ref[:, hs]
            ym = _sigmoid(o_pre[rs, hs]) * hn
            ycat[rs, D_POOL + h * HEAD_DIM:D_POOL + (h + 1) * HEAD_DIM] = ym.astype(BF16)

            g_tot = bc[L - 1:L, :]
            m_loc = g_tot + cm_c[L - 1:L, :]
            wa = jnp.exp(g_tot + ic - bc - m_loc)
            wv2 = (wa * v2.astype(F32)).astype(BF16)
            st_loc = lax.dot_general(k_h, wv2, dn_tn, preferred_element_type=F32)
            m_new = jnp.maximum(g_tot + m_prev, m_loc)
            state[h] = jnp.exp(g_tot + m_prev - m_new) * state[h] + jnp.exp(m_loc - m_new) * st_loc
            m_state[h] = m_new
        mix = _dot(ycat[rs, :], wout_ref[...])
        h_ref[rs, :] = _layer_norm_rows(ALPHA * x_ref[rs, :] + mix, lg_ref[...], lb_ref[...])
    for h in range(N_HEADS):
        st_s[h] = state[h]
        m_s[h:h + 1, :] = jnp.broadcast_to(m_state[h], (1, LANES))


def _mixer_layer(x2d, bsz, seq, w_in, b_gates, conv_w, conv_b, pool_w, pool_scale, mlstm_g, w_out,
                 ln_g, ln_b):
    ts = MIX_TS
    assert seq % ts == 0 and ts % MIX_L == 0
    n_s_tiles = seq // ts
    w_in_p = jnp.pad(w_in, ((0, 0), (0, D_IN_PAD - D_IN))).astype(BF16)
    bg_p = jnp.pad(b_gates, (0, LANES - N_GATES)).reshape(1, LANES)
    row = lambda a: a.reshape(1, -1)
    const2 = lambda b, s: (0, 0)
    const3 = lambda b, s: (0, 0, 0)
    return pl.pallas_call(
        _mixer_kernel,
        grid=(bsz, n_s_tiles),
        in_specs=[
            pl.BlockSpec((ts, D_MODEL), lambda b, s: (b * n_s_tiles + s, 0)),
            pl.BlockSpec((D_MODEL, D_IN_PAD), const2),
            pl.BlockSpec((1, LANES), const2),
            pl.BlockSpec((CONV_WIDTH, 2 * D_MLSTM), const2),
            pl.BlockSpec((1, 2 * D_MLSTM), const2),
            pl.BlockSpec((len(POOL_WINDOWS), POOL_GROUP, POOL_GROUP), const3),
            pl.BlockSpec((1, D_POOL), const2),
            pl.BlockSpec((1, D_MLSTM), const2),
            pl.BlockSpec((D_MODEL, D_MODEL), const2),
            pl.BlockSpec((1, D_MODEL), const2),
            pl.BlockSpec((1, D_MODEL), const2),
        ],
        out_specs=pl.BlockSpec((ts, D_MODEL), lambda b, s: (b * n_s_tiles + s, 0)),
        out_shape=jax.ShapeDtypeStruct((bsz * seq, D_MODEL), F32),
        scratch_shapes=[
            pltpu.VMEM((POOL_PAD + ts, D_POOL), F32),
            pltpu.VMEM((CONV_PAD + ts, 2 * D_MLSTM), F32),
            pltpu.VMEM((ts, D_MLSTM), BF16),
            pltpu.VMEM((ts, D_MLSTM), BF16),
            pltpu.VMEM((ts, D_MODEL), BF16),
            pltpu.VMEM((N_HEADS, HEAD_DIM, 2 * HEAD_DIM), F32),
            pltpu.VMEM((SUBLANES, LANES), F32),
        ],
        compiler_params=pltpu.CompilerParams(
            dimension_semantics=("arbitrary", "arbitrary"),
            vmem_limit_bytes=VMEM_LIMIT_BYTES),
        name="mixer",
    )(x2d, w_in_p, bg_p, conv_w, row(conv_b), pool_w.astype(BF16), row(pool_scale), row(mlstm_g),
      w_out.astype(BF16), row(ln_g), row(ln_b))


DENSE_TM = 512
FF_CHUNK = 512


def _swiglu_hidden(xb, wg, wu, a_s, before_chunk=None):
    d_ff = a_s.shape[1]
    for c, c0 in enumerate(range(0, d_ff, FF_CHUNK)):
        if before_chunk is not None:
            before_chunk(c)
        cs = slice(c0, min(c0 + FF_CHUNK, d_ff))
        g = _dot(xb, wg[:, cs])
        u = _dot(xb, wu[:, cs])
        a_s[:, cs] = (g * _sigmoid(g) * u).astype(BF16)


def _ffn_dense_kernel(x_ref, wg_ref, wu_ref, wd_ref, lg_ref, lb_ref, o_ref, a_s):
    x = x_ref[...]
    _swiglu_hidden(x.astype(BF16), wg_ref, wu_ref, a_s)
    f = _dot(a_s[...], wd_ref[...])
    o_ref[...] = _layer_norm_rows(ALPHA * x + f, lg_ref[...], lb_ref[...])


def _ffn_dense_layer(h2d, wg, wu, wd, ln_g, ln_b):
    n_tok = h2d.shape[0]
    d_ff = wg.shape[1]
    tm = DENSE_TM
    assert n_tok % tm == 0
    row = lambda a: a.reshape(1, -1)
    resident = dict(pipeline_mode=pl.Buffered(1))
    return pl.pallas_call(
        _ffn_dense_kernel,
        grid=(n_tok // tm,),
        in_specs=[
            pl.BlockSpec((tm, D_MODEL), lambda i: (i, 0)),
            pl.BlockSpec((D_MODEL, d_ff), lambda i: (0, 0), **resident),
            pl.BlockSpec((D_MODEL, d_ff), lambda i: (0, 0), **resident),
            pl.BlockSpec((d_ff, D_MODEL), lambda i: (0, 0), **resident),
            pl.BlockSpec((1, D_MODEL), lambda i: (0, 0)),
            pl.BlockSpec((1, D_MODEL), lambda i: (0, 0)),
        ],
        out_specs=pl.BlockSpec((tm, D_MODEL), lambda i: (i, 0)),
        out_shape=jax.ShapeDtypeStruct((n_tok, D_MODEL), F32),
        scratch_shapes=[pltpu.VMEM((tm, d_ff), BF16)],
        compiler_params=pltpu.CompilerParams(
            dimension_semantics=("arbitrary",), vmem_limit_bytes=VMEM_LIMIT_BYTES),
        name="ffn_dense",
    )(h2d, wg.astype(BF16), wu.astype(BF16), wd.astype(BF16), row(ln_g), row(ln_b))


ROUTE_TS = 512
MOE_TM = 512
ROUTE_COLS = 8
INVERT_STEPS = 16


def _route_kernel(h_ref, wr_ref, br_ref, idx_ref, wts_ref, cnt_ref, carry_s):
    ts = ROUTE_TS
    i = pl.program_id(0)

    @pl.when(i == 0)
    def _():
        carry_s[...] = jnp.zeros_like(carry_s)

    h = h_ref[...]
    h_hi = h.astype(BF16)
    h_lo = (h - h_hi.astype(F32)).astype(BF16)
    logits = (_dot(h_hi, wr_ref[0]) + _dot(h_lo, wr_ref[0]) + _dot(h_hi, wr_ref[1])) + br_ref[...]
    lane = lax.broadcasted_iota(jnp.int32, (ts, LANES), 1)
    lg = jnp.where(lane < N_EXPERTS, logits, -jnp.inf)
    m1 = jnp.max(lg, axis=1, keepdims=True)
    i1 = jnp.min(jnp.where(lg == m1, lane, LANES), axis=1, keepdims=True)
    oh1 = lane == i1
    lg2 = jnp.where(oh1, -jnp.inf, lg)
    m2 = jnp.max(lg2, axis=1, keepdims=True)
    i2 = jnp.min(jnp.where(lg2 == m2, lane, LANES), axis=1, keepdims=True)
    oh2 = lane == i2
    e2 = jnp.exp(m2 - m1)
    w1 = 1.0 / (1.0 + e2)
    w2 = e2 / (1.0 + e2)

    sel = jnp.where(oh1 | oh2, 1.0, 0.0)
    strict_lower = jnp.where(lax.broadcasted_iota(jnp.int32, (ts, ts), 1)
                             < lax.broadcasted_iota(jnp.int32, (ts, ts), 0), 1.0, 0.0).astype(BF16)
    before = _dot(strict_lower, sel.astype(BF16)) + carry_s[...]
    r1 = jnp.sum(jnp.where(oh1, before, 0.0), axis=1, keepdims=True).astype(jnp.int32)
    r2 = jnp.sum(jnp.where(oh2, before, 0.0), axis=1, keepdims=True).astype(jnp.int32)
    carry_s[...] += jnp.sum(sel, axis=0, keepdims=True)

    idx_cols = jnp.where(lane == 0, i1, jnp.where(lane == 1, i2, jnp.where(lane == 2, r1, r2)))
    idx_ref[...] = idx_cols.T[0:ROUTE_COLS, :]
    col = lax.broadcasted_iota(jnp.int32, (ts, ROUTE_COLS), 1)
    wts_ref[...] = jnp.where(col == 0, w1, w2)
    cnt_ref[...] = carry_s[...]


def _route(h2d, w_router, b_router):
    n_tok = h2d.shape[0]
    ts = ROUTE_TS
    assert n_tok % ts == 0
    wr = jnp.pad(w_router, ((0, 0), (0, LANES - N_EXPERTS)))
    wr_hi = wr.astype(BF16)
    wr = jnp.stack([wr_hi, (wr - wr_hi.astype(F32)).astype(BF16)])
    br = jnp.pad(b_router, (0, LANES - N_EXPERTS)).reshape(1, LANES)
    return pl.pallas_call(
        _route_kernel,
        grid=(n_tok // ts,),
        in_specs=[
            pl.BlockSpec((ts, D_MODEL), lambda i: (i, 0)),
            pl.BlockSpec((2, D_MODEL, LANES), lambda i: (0, 0, 0)),
            pl.BlockSpec((1, LANES), lambda i: (0, 0)),
        ],
        out_specs=[
            pl.BlockSpec((ROUTE_COLS, ts), lambda i: (0, i)),
            pl.BlockSpec((ts, ROUTE_COLS), lambda i: (i, 0)),
            pl.BlockSpec((1, LANES), lambda i: (0, 0)),
        ],
        out_shape=[
            jax.ShapeDtypeStruct((ROUTE_COLS, n_tok), jnp.int32),
            jax.ShapeDtypeStruct((n_tok, ROUTE_COLS), F32),
            jax.ShapeDtypeStruct((1, LANES), F32),
        ],
        scratch_shapes=[pltpu.VMEM((1, LANES), F32)],
        compiler_params=pltpu.CompilerParams(
            dimension_semantics=("arbitrary",), vmem_limit_bytes=VMEM_LIMIT_BYTES),
        name="moe_route",
    )(h2d, wr, br)


def _row_copy(src_ref, src_row, dst_ref, dst_row, sem):
    return pltpu.make_async_copy(src_ref.at[pl.ds(src_row, 1), :], dst_ref.at[pl.ds(dst_row, 1), :], sem)


def _invert_kernel(pos1_ref, pos2_ref, default_hbm, code_ref, sem):
    n_tok = pos1_ref.shape[0]
    s = pl.program_id(0)
    toks_per_step = n_tok // INVERT_STEPS

    @pl.when(s == 0)
    def _():
        fill = pltpu.make_async_copy(default_hbm, code_ref, sem)
        fill.start()
        fill.wait()

    base = s * toks_per_step

    def scatter(j, carry):
        t = base + j
        code_ref[pos1_ref[t]] = t
        code_ref[pos2_ref[t]] = n_tok + t
        return carry

    lax.fori_loop(0, toks_per_step, scatter, 0, unroll=8)


def _invert(pos1, pos2, n_code):
    n_tok = pos1.shape[0]
    assert n_tok % INVERT_STEPS == 0
    smem = pl.BlockSpec(memory_space=pltpu.SMEM)
    default = 2 * n_tok + jnp.arange(n_code, dtype=jnp.int32)
    return pl.pallas_call(
        _invert_kernel,
        grid=(INVERT_STEPS,),
        in_specs=[smem, smem, pl.BlockSpec(memory_space=pl.ANY)],
        out_specs=smem,
        out_shape=jax.ShapeDtypeStruct((n_code,), jnp.int32),
        scratch_shapes=[pltpu.SemaphoreType.DMA],
        compiler_params=pltpu.CompilerParams(dimension_semantics=("arbitrary",)),
        name="moe_invert",
    )(pos1, pos2, default)


def _ffn_moe_kernel(layer, te_ref, nu_ref, code_ref, h_hbm, wg_hbm, wu_hbm, wd_hbm, y_hbm,
                    xbuf, xb_s, a_s, obuf, wg_s, wu_s, wd_s, stage_in, stage_out, gsem, ssem, wsem):
    tm = MOE_TM
    n_tok = h_hbm.shape[0]
    d_ff = a_s.shape[1]
    n_chunks = d_ff // FF_CHUNK
    i = pl.program_id(0)
    n_used = nu_ref[0]

    def stream_cast(src_chunk, store_chunk, stage):
        copy = lambda c: pltpu.make_async_copy(src_chunk(c), stage.at[c % 2], wsem.at[c % 2])
        copy(0).start()
        for c in range(n_chunks):
            if c + 1 < n_chunks:
                copy(c + 1).start()
            copy(c).wait()
            store_chunk(c, stage[c % 2].astype(BF16))

    def load_expert(e):
        col = lambda c: pl.ds(c * FF_CHUNK, FF_CHUNK)

        def store_cols(dst):
            def store(c, val):
                dst[:, c * FF_CHUNK:(c + 1) * FF_CHUNK] = val
            return store

        def store_rows(c, val):
            wd_s[c * FF_CHUNK:(c + 1) * FF_CHUNK, :] = val

        stream_cast(lambda c: wg_hbm.at[layer, e, :, col(c)], store_cols(wg_s), stage_in)
        stream_cast(lambda c: wu_hbm.at[layer, e, :, col(c)], store_cols(wu_s), stage_in)
        stream_cast(lambda c: wd_hbm.at[layer, e, col(c), :], store_rows, stage_out)

    @pl.when((i < n_used) & ((i == 0) | (te_ref[i] != te_ref[jnp.maximum(i - 1, 0)])))
    def _():
        load_expert(te_ref[i])

    def gather_copy(r, code):
        tok = code & (n_tok - 1) if n_tok & (n_tok - 1) == 0 else lax.rem(code, n_tok)
        return _row_copy(h_hbm, tok, xbuf, r, gsem)

    def scatter_copy(r, code):
        return _row_copy(obuf, r, y_hbm, code, ssem)

    def issue_inline(make_copy, tile, part):
        n_parts = max(n_chunks - 2, 1)
        if part < n_parts:
            for r in range(part * tm // n_parts, (part + 1) * tm // n_parts):
                make_copy(r, code_ref[tile * tm + r]).start(priority=r % 2)

    def issue_loop(make_copy, tile):
        def body(r, carry):
            make_copy(r, code_ref[tile * tm + r]).start()
            return carry

        lax.fori_loop(0, tm, body, 0, unroll=8)

    def drain(make_copy):
        def body(r, carry):
            make_copy(r, 0).wait()
            return carry

        lax.fori_loop(0, tm, body, 0, unroll=8)

    def step(with_scatter):
        drain(gather_copy)
        xb_s[...] = xbuf[...].astype(BF16)

        def before_chunk(c):
            issue_inline(gather_copy, i + 1, c)
            if with_scatter:
                issue_inline(scatter_copy, i - 1, c)

        _swiglu_hidden(xb_s[...], wg_s, wu_s, a_s, before_chunk)
        if with_scatter:
            drain(scatter_copy)
        obuf[...] = _dot(a_s[...], wd_s[...])

    @pl.when(i == 0)
    def _():
        issue_loop(gather_copy, 0)
        step(False)

    @pl.when((i > 0) & (i < n_used))
    def _():
        step(True)

    @pl.when(i == n_used)
    def _():
        drain(gather_copy)
        issue_loop(scatter_copy, i - 1)
        drain(scatter_copy)


def _ffn_moe(h2d, tile_expert, n_used, code, n_y_rows, layer, wg_all, wu_all, wd_all):
    d_ff = wg_all.shape[3]
    tm = MOE_TM
    n_steps = tile_expert.shape[0]
    assert code.shape[0] == n_steps * tm and d_ff % FF_CHUNK == 0
    hbm = pl.BlockSpec(memory_space=pl.ANY)
    return pl.pallas_call(
        functools.partial(_ffn_moe_kernel, layer),
        grid_spec=pltpu.PrefetchScalarGridSpec(
            num_scalar_prefetch=3,
            grid=(n_steps,),
            in_specs=[hbm, hbm, hbm, hbm],
            out_specs=hbm,
            scratch_shapes=[
                pltpu.VMEM((tm, D_MODEL), F32),
                pltpu.VMEM((tm, D_MODEL), BF16),
                pltpu.VMEM((tm, d_ff), BF16),
                pltpu.VMEM((tm, D_MODEL), F32),
                pltpu.VMEM((D_MODEL, d_ff), BF16),
                pltpu.VMEM((D_MODEL, d_ff), BF16),
                pltpu.VMEM((d_ff, D_MODEL), BF16),
                pltpu.VMEM((2, D_MODEL, FF_CHUNK), F32),
                pltpu.VMEM((2, FF_CHUNK, D_MODEL), F32),
                pltpu.SemaphoreType.DMA,
                pltpu.SemaphoreType.DMA,
                pltpu.SemaphoreType.DMA((2,)),
            ],
        ),
        out_shape=jax.ShapeDtypeStruct((n_y_rows, D_MODEL), F32),
        compiler_params=pltpu.CompilerParams(
            dimension_semantics=("arbitrary",), vmem_limit_bytes=VMEM_LIMIT_BYTES),
        name="ffn_moe",
    )(tile_expert, n_used, code, h2d, wg_all, wu_all, wd_all)


COMBINE_TS = 512


def _combine_kernel(h_ref, wts_ref, y1_ref, y2_ref, lg_ref, lb_ref, o_ref):
    w = wts_ref[...]
    f = w[:, 0:1] * y1_ref[...] + w[:, 1:2] * y2_ref[...]
    o_ref[...] = _layer_norm_rows(ALPHA * h_ref[...] + f, lg_ref[...], lb_ref[...])


def _combine(h2d, wts, y, ln_g, ln_b):
    n_tok = h2d.shape[0]
    ts = COMBINE_TS
    assert n_tok % ts == 0
    row = lambda a: a.reshape(1, -1)
    second = n_tok // ts
    return pl.pallas_call(
        _combine_kernel,
        grid=(n_tok // ts,),
        in_specs=[
            pl.BlockSpec((ts, D_MODEL), lambda i: (i, 0)),
            pl.BlockSpec((ts, ROUTE_COLS), lambda i: (i, 0)),
            pl.BlockSpec((ts, D_MODEL), lambda i: (i, 0)),
            pl.BlockSpec((ts, D_MODEL), lambda i: (second + i, 0)),
            pl.BlockSpec((1, D_MODEL), lambda i: (0, 0)),
            pl.BlockSpec((1, D_MODEL), lambda i: (0, 0)),
        ],
        out_specs=pl.BlockSpec((ts, D_MODEL), lambda i: (i, 0)),
        out_shape=jax.ShapeDtypeStruct((n_tok, D_MODEL), F32),
        compiler_params=pltpu.CompilerParams(
            dimension_semantics=("arbitrary",), vmem_limit_bytes=VMEM_LIMIT_BYTES),
        name="moe_combine",
    )(h2d, wts, y, y, row(ln_g), row(ln_b))


def _moe_layer(h2d, w_router, b_router, layer, wg_all, wu_all, wd_all, ln_g, ln_b):
    n_tok = h2d.shape[0]
    tm = MOE_TM
    idx, wts, cnt = _route(h2d, w_router, b_router)

    counts = cnt[0, :N_EXPERTS].astype(jnp.int32)
    padded = ((counts + tm - 1) // tm) * tm
    ends = jnp.cumsum(padded)
    starts = ends - padded
    n_steps = (TOP_K * n_tok) // tm + N_EXPERTS + 1
    n_steps += n_steps % 2
    n_used = (ends[-1] // tm).reshape(1)
    tile_id = jnp.minimum(jnp.arange(n_steps, dtype=jnp.int32), n_used - 1)
    tile_expert = jnp.minimum(
        jnp.sum((ends[None, :] <= (tile_id * tm)[:, None]).astype(jnp.int32), axis=1), N_EXPERTS - 1)
    start_of = lambda e: sum(jnp.where(e == k, starts[k], 0) for k in range(N_EXPERTS))
    pos1 = start_of(idx[0]) + idx[2]
    pos2 = start_of(idx[1]) + idx[3]

    n_code = n_steps * tm
    code = _invert(pos1, pos2, n_code)
    y = _ffn_moe(h2d, tile_expert, n_used, code, TOP_K * n_tok + n_code, layer, wg_all, wu_all, wd_all)
    return _combine(h2d, wts, y, ln_g, ln_b)


def kernel(x, w_in, b_gates, conv_w, conv_b, pool_w, pool_scale, mlstm_g, w_out, ln1_g, ln1_b, ln2_g, ln2_b, wg_dense, wu_dense, wd_dense, w_router, b_router, wg_exp, wu_exp, wd_exp):
    bsz, seq, d = x.shape
    x2d = x.reshape(bsz * seq, d)
    for l in range(DEPTH):
        h2d = _mixer_layer(x2d, bsz, seq, w_in[l], b_gates[l], conv_w[l], conv_b[l], pool_w[l],
                           pool_scale[l], mlstm_g[l], w_out[l], ln1_g[l], ln1_b[l])
        j = l // 2
        if l % 2 == 0:
            x2d = _ffn_dense_layer(h2d, wg_dense[j], wu_dense[j], wd_dense[j], ln2_g[l], ln2_b[l])
        else:
            x2d = _moe_layer(h2d, w_router[j], b_router[j], j, wg_exp, wu_exp, wd_exp,
                             ln2_g[l], ln2_b[l])
    return x2d.reshape(bsz, seq, d)
```

```python
import functools

import jax
import jax.numpy as jnp
from jax import lax
from jax.experimental import pallas as pl
from jax.experimental.pallas import tpu as pltpu

F32 = jnp.float32
BF16 = jnp.bfloat16

D_MODEL = 1024
DEPTH = 4
D_POOL = 512
POOL_WINDOWS = (2, 4, 8, 16)
POOL_GROUP = 128
MAX_WINDOW = max(POOL_WINDOWS)
D_MLSTM = 512
N_HEADS = 4
HEAD_DIM = 128
CONV_WIDTH = 4
N_GATES = 2 * N_HEADS
D_IN = D_POOL + 4 * D_MLSTM + N_GATES
N_EXPERTS = 8
TOP_K = 2
ALPHA = (2 * DEPTH) ** 0.25
LN_EPS = 1e-5

LANES = 128
SUBLANES = 8
D_IN_PAD = D_POOL + 4 * D_MLSTM + LANES
VMEM_LIMIT_BYTES = 56 * 1024 * 1024

OFF_QK = D_POOL
OFF_V = OFF_QK + 2 * D_MLSTM
OFF_O = OFF_V + D_MLSTM
OFF_G = OFF_O + D_MLSTM

MIX_TS = 1024
MIX_L = 256
CONV_PAD = SUBLANES
POOL_PAD = 16


def _dot(a, b):
    return jnp.dot(a, b, preferred_element_type=F32)


def _sigmoid(x):
    return 1.0 / (1.0 + jnp.exp(-x))


def _layer_norm_rows(r, g, b):
    mu = jnp.mean(r, axis=-1, keepdims=True)
    rc = r - mu
    var = jnp.mean(rc * rc, axis=-1, keepdims=True)
    return rc * lax.rsqrt(var + LN_EPS) * g + b


def _mixer_kernel(x_ref, win_ref, bg_ref, cw_ref, cb_ref, pw_ref, ps_ref, mg_ref, wout_ref,
                  lg_ref, lb_ref, h_ref,
                  pbuf, cbuf, q_s, k_s, ycat, st_s, m_s):
    ts, L = MIX_TS, MIX_L
    s_idx = pl.program_id(1)

    @pl.when(s_idx == 0)
    def _():
        pbuf[0:POOL_PAD, :] = jnp.zeros((POOL_PAD, D_POOL), F32)
        cbuf[0:CONV_PAD, :] = jnp.zeros((CONV_PAD, 2 * D_MLSTM), F32)
        st_s[...] = jnp.zeros_like(st_s)
        m_s[...] = jnp.zeros_like(m_s)

    xb = x_ref[...].astype(BF16)
    gates = _dot(xb, win_ref[:, OFF_G:D_IN_PAD]) + bg_ref[...]

    g_row = gates.T[0:N_GATES, :]
    logf_row = jnp.minimum(g_row, 0.0) - jnp.log1p(jnp.exp(-jnp.abs(g_row)))
    lane_in_chunk = lax.broadcasted_iota(jnp.int32, (N_GATES, ts), 1) % L
    b_row = logf_row
    sh = 1
    while sh < L:
        b_row = b_row + jnp.where(lane_in_chunk >= sh, pltpu.roll(b_row, sh, axis=1), 0.0)
        sh *= 2
    b_heads = b_row[N_HEADS:N_GATES, :]
    a_row = g_row[0:N_HEADS, :] - b_heads
    cmax_row = a_row
    sh = 1
    while sh < L:
        cmax_row = jnp.maximum(cmax_row, jnp.where(lane_in_chunk[0:N_HEADS] >= sh,
                                                   pltpu.roll(cmax_row, sh, axis=1), -jnp.inf))
        sh *= 2
    cols = jnp.concatenate([cmax_row, b_heads, jnp.zeros((LANES - N_GATES, ts), F32)], axis=0).T

    cbuf[CONV_PAD:CONV_PAD + ts, :] = _dot(xb, win_ref[:, OFF_QK:OFF_V])
    pbuf[POOL_PAD:POOL_PAD + ts, :] = _dot(xb, win_ref[:, 0:OFF_QK])
    v = _dot(xb, win_ref[:, OFF_V:OFF_O]).astype(BF16)
    o_pre = _dot(xb, win_ref[:, OFF_O:OFF_G])

    for c in range(2 * D_MLSTM // LANES):
        cs = slice(c * LANES, (c + 1) * LANES)
        y = cb_ref[:, cs] + cbuf[CONV_PAD:CONV_PAD + ts, cs] * cw_ref[CONV_WIDTH - 1:CONV_WIDTH, cs]
        for j in range(1, CONV_WIDTH):
            y = y + cbuf[CONV_PAD - j:CONV_PAD - j + ts, cs] * cw_ref[CONV_WIDTH - 1 - j:CONV_WIDTH - j, cs]
        y = y * _sigmoid(y)
        if c < N_HEADS:
            q_s[:, cs] = y.astype(BF16)
        else:
            ks = slice((c - N_HEADS) * LANES, (c - N_HEADS + 1) * LANES)
            k_s[:, ks] = (y * (HEAD_DIM ** -0.5)).astype(BF16)
    cbuf[0:CONV_PAD, :] = cbuf[ts:ts + CONV_PAD, :]

    t_glob = lax.broadcasted_iota(jnp.int32, (ts, 1), 0) + s_idx * ts
    for g, w in enumerate(POOL_WINDOWS):
        cs = slice(g * POOL_GROUP, (g + 1) * POOL_GROUP)
        acc = pbuf[:, cs]
        span = 1
        while span < w:
            acc = acc + pltpu.roll(acc, span, axis=0)
            span *= 2
        acc = acc[POOL_PAD:POOL_PAD + ts, :]
        u = pbuf[POOL_PAD:POOL_PAD + ts, cs]
        cnt = jnp.minimum(t_glob + 1, w).astype(F32)
        d = acc / cnt - u
        y = _dot(d.astype(BF16), pw_ref[g]) * ps_ref[:, cs]
        ycat[:, cs] = y.astype(BF16)
    pbuf[0:POOL_PAD, :] = pbuf[ts:ts + POOL_PAD, :]

    tri = (lax.broadcasted_iota(jnp.int32, (L, L), 1) <= lax.broadcasted_iota(jnp.int32, (L, L), 0))
    dn_nt = (((1,), (1,)), ((), ()))
    dn_tn = (((0,), (0,)), ((), ()))
    ones_blk = jnp.ones((L, HEAD_DIM), BF16)

    state = [st_s[h] for h in range(N_HEADS)]
    m_state = [m_s[h:h + 1, 0:1] for h in range(N_HEADS)]
    for c in range(ts // L):
        rs = slice(c * L, (c + 1) * L)
        for h in range(N_HEADS):
            hs = slice(h * HEAD_DIM, (h + 1) * HEAD_DIM)
            cm_c = cols[rs, h:h + 1]
            bc = cols[rs, N_HEADS + h:N_HEADS + h + 1]
            ic = gates[rs, h:h + 1]
            ar = a_row[h:h + 1, rs]
            m_prev = m_state[h]
            q_h = q_s[rs, hs]
            k_h = k_s[rs, hs]
            v2 = jnp.concatenate([v[rs, hs], ones_blk], axis=1)

            mm = jnp.maximum(cm_c, m_prev)
            s_qk = lax.dot_general(q_h, k_h, dn_nt, preferred_element_type=F32)
            wmat = jnp.exp(jnp.where(tri, ar - mm, -jnp.inf)) * s_qk
            s_inter = jnp.exp(m_prev - mm)
            tot = _dot(wmat.astype(BF16), v2) + s_inter * _dot(q_h, state[h].astype(BF16))
            num = tot[:, 0:HEAD_DIM]
            den = tot[:, HEAD_DIM:2 * HEAD_DIM]
            hh = num / jnp.maximum(jnp.abs(den), jnp.exp(-(bc + mm)))

            mu = jnp.mean(hh, axis=1, keepdims=True)
            hc = hh - mu
            var = jnp.mean(hc * hc, axis=1, keepdims=True)
            hn = hc * lax.rsqrt(var + LN_EPS) * mg_ref[:, hs]
            ym = _sigmoid(o_pre[rs, hs]) * hn
            ycat[rs, D_POOL + h * HEAD_DIM:D_POOL + (h + 1) * HEAD_DIM] = ym.astype(BF16)

            g_tot = bc[L - 1:L, :]
            m_loc = g_tot + cm_c[L - 1:L, :]
            wa = jnp.exp(g_tot + ic - bc - m_loc)
            wv2 = (wa * v2.astype(F32)).astype(BF16)
            st_loc = lax.dot_general(k_h, wv2, dn_tn, preferred_element_type=F32)
            m_new = jnp.maximum(g_tot + m_prev, m_loc)
            state[h] = jnp.exp(g_tot + m_prev - m_new) * state[h] + jnp.exp(m_loc - m_new) * st_loc
            m_state[h] = m_new
        mix = _dot(ycat[rs, :], wout_ref[...])
        h_ref[rs, :] = _layer_norm_rows(ALPHA * x_ref[rs, :] + mix, lg_ref[...], lb_ref[...])
    for h in range(N_HEADS):
        st_s[h] = state[h]
        m_s[h:h + 1, :] = jnp.broadcast_to(m_state[h], (1, LANES))


def _mixer_layer(x2d, bsz, seq, w_in, b_gates, conv_w, conv_b, pool_w, pool_scale, mlstm_g, w_out,
                 ln_g, ln_b):
    ts = MIX_TS
    assert seq % ts == 0 and ts % MIX_L == 0
    n_s_tiles = seq // ts
    w_in_p = jnp.pad(w_in, ((0, 0), (0, D_IN_PAD - D_IN))).astype(BF16)
    bg_p = jnp.pad(b_gates, (0, LANES - N_GATES)).reshape(1, LANES)
    row = lambda a: a.reshape(1, -1)
    const2 = lambda b, s: (0, 0)
    const3 = lambda b, s: (0, 0, 0)
    return pl.pallas_call(
        _mixer_kernel,
        grid=(bsz, n_s_tiles),
        in_specs=[
            pl.BlockSpec((ts, D_MODEL), lambda b, s: (b * n_s_tiles + s, 0)),
            pl.BlockSpec((D_MODEL, D_IN_PAD), const2),
            pl.BlockSpec((1, LANES), const2),
            pl.BlockSpec((CONV_WIDTH, 2 * D_MLSTM), const2),
            pl.BlockSpec((1, 2 * D_MLSTM), const2),
            pl.BlockSpec((len(POOL_WINDOWS), POOL_GROUP, POOL_GROUP), const3),
            pl.BlockSpec((1, D_POOL), const2),
            pl.BlockSpec((1, D_MLSTM), const2),
            pl.BlockSpec((D_MODEL, D_MODEL), const2),
            pl.BlockSpec((1, D_MODEL), const2),
            pl.BlockSpec((1, D_MODEL), const2),
        ],
        out_specs=pl.BlockSpec((ts, D_MODEL), lambda b, s: (b * n_s_tiles + s, 0)),
        out_shape=jax.ShapeDtypeStruct((bsz * seq, D_MODEL), F32),
        scratch_shapes=[
            pltpu.VMEM((POOL_PAD + ts, D_POOL), F32),
            pltpu.VMEM((CONV_PAD + ts, 2 * D_MLSTM), F32),
            pltpu.VMEM((ts, D_MLSTM), BF16),
            pltpu.VMEM((ts, D_MLSTM), BF16),
            pltpu.VMEM((ts, D_MODEL), BF16),
            pltpu.VMEM((N_HEADS, HEAD_DIM, 2 * HEAD_DIM), F32),
            pltpu.VMEM((SUBLANES, LANES), F32),
        ],
        compiler_params=pltpu.CompilerParams(
            dimension_semantics=("arbitrary", "arbitrary"),
            vmem_limit_bytes=VMEM_LIMIT_BYTES),
        name="mixer",
    )(x2d, w_in_p, bg_p, conv_w, row(conv_b), pool_w.astype(BF16), row(pool_scale), row(mlstm_g),
      w_out.astype(BF16), row(ln_g), row(ln_b))


DENSE_TM = 1024
FF_CHUNK = 512


def _swiglu_hidden(xb, wg, wu, a_s, before_chunk=None):
    d_ff = a_s.shape[1]
    for c, c0 in enumerate(range(0, d_ff, FF_CHUNK)):
        if before_chunk is not None:
            before_chunk(c)
        cs = slice(c0, min(c0 + FF_CHUNK, d_ff))
        g = _dot(xb, wg[:, cs])
        u = _dot(xb, wu[:, cs])
        a_s[:, cs] = (g * _sigmoid(g) * u).astype(BF16)


def _ffn_dense_kernel(x_ref, wg_ref, wu_ref, wd_ref, lg_ref, lb_ref, o_ref, a_s):
    x = x_ref[...]
    _swiglu_hidden(x.astype(BF16), wg_ref, wu_ref, a_s)
    f = _dot(a_s[...], wd_ref[...])
    o_ref[...] = _layer_norm_rows(ALPHA * x + f, lg_ref[...], lb_ref[...])


def _ffn_dense_layer(h2d, wg, wu, wd, ln_g, ln_b):
    n_tok = h2d.shape[0]
    d_ff = wg.shape[1]
    tm = DENSE_TM
    assert n_tok % tm == 0
    row = lambda a: a.reshape(1, -1)
    resident = dict(pipeline_mode=pl.Buffered(1))
    return pl.pallas_call(
        _ffn_dense_kernel,
        grid=(n_tok // tm,),
        in_specs=[
            pl.BlockSpec((tm, D_MODEL), lambda i: (i, 0)),
            pl.BlockSpec((D_MODEL, d_ff), lambda i: (0, 0), **resident),
            pl.BlockSpec((D_MODEL, d_ff), lambda i: (0, 0), **resident),
            pl.BlockSpec((d_ff, D_MODEL), lambda i: (0, 0), **resident),
            pl.BlockSpec((1, D_MODEL), lambda i: (0, 0)),
            pl.BlockSpec((1, D_MODEL), lambda i: (0, 0)),
        ],
        out_specs=pl.BlockSpec((tm, D_MODEL), lambda i: (i, 0)),
        out_shape=jax.ShapeDtypeStruct((n_tok, D_MODEL), F32),
        scratch_shapes=[pltpu.VMEM((tm, d_ff), BF16)],
        compiler_params=pltpu.CompilerParams(
            dimension_semantics=("arbitrary",), vmem_limit_bytes=VMEM_LIMIT_BYTES),
        name="ffn_dense",
    )(h2d, wg.astype(BF16), wu.astype(BF16), wd.astype(BF16), row(ln_g), row(ln_b))


ROUTE_TS = 512
MOE_TM = 512
ROUTE_COLS = 8
INVERT_STEPS = 16


def _route_kernel(h_ref, wr_ref, br_ref, idx_ref, wts_ref, cnt_ref, carry_s):
    ts = ROUTE_TS
    i = pl.program_id(0)

    @pl.when(i == 0)
    def _():
        carry_s[...] = jnp.zeros_like(carry_s)

    h = h_ref[...]
    h_hi = h.astype(BF16)
    h_lo = (h - h_hi.astype(F32)).astype(BF16)
    logits = (_dot(h_hi, wr_ref[0]) + _dot(h_lo, wr_ref[0]) + _dot(h_hi, wr_ref[1])) + br_ref[...]
    lg = logits.T[0:N_EXPERTS, :]
    expert = lax.broadcasted_iota(jnp.int32, (N_EXPERTS, ts), 0)
    m1 = jnp.max(lg, axis=0, keepdims=True)
    i1 = jnp.min(jnp.where(lg == m1, expert, N_EXPERTS), axis=0, keepdims=True)
    oh1 = expert == i1
    lg2 = jnp.where(oh1, -jnp.inf, lg)
    m2 = jnp.max(lg2, axis=0, keepdims=True)
    i2 = jnp.min(jnp.where(lg2 == m2, expert, N_EXPERTS), axis=0, keepdims=True)
    oh2 = expert == i2
    e2 = jnp.exp(m2 - m1)
    w1 = 1.0 / (1.0 + e2)
    w2 = e2 / (1.0 + e2)

    sel = jnp.where(oh1 | oh2, 1.0, 0.0)
    strict_upper = jnp.where(lax.broadcasted_iota(jnp.int32, (ts, ts), 0)
                             < lax.broadcasted_iota(jnp.int32, (ts, ts), 1), 1.0, 0.0).astype(BF16)
    before = _dot(sel.astype(BF16), strict_upper) + carry_s[:, 0:1]
    r1 = jnp.sum(jnp.where(oh1, before, 0.0), axis=0, keepdims=True).astype(jnp.int32)
    r2 = jnp.sum(jnp.where(oh2, before, 0.0), axis=0, keepdims=True).astype(jnp.int32)
    carry_s[...] += jnp.sum(sel, axis=1, keepdims=True)

    idx_ref[...] = jnp.concatenate(
        [i1, i2, r1, r2, jnp.zeros((ROUTE_COLS - 4, ts), jnp.int32)], axis=0)
    w_rows = jnp.concatenate([w1, w2, jnp.zeros((LANES - 2, ts), F32)], axis=0)
    wts_ref[...] = w_rows.T[:, 0:ROUTE_COLS]
    cnt_ref[...] = carry_s[...]


def _route(h2d, w_router, b_router):
    n_tok = h2d.shape[0]
    ts = ROUTE_TS
    assert n_tok % ts == 0
    wr = jnp.pad(w_router, ((0, 0), (0, LANES - N_EXPERTS)))
    wr_hi = wr.astype(BF16)
    wr = jnp.stack([wr_hi, (wr - wr_hi.astype(F32)).astype(BF16)])
    br = jnp.pad(b_router, (0, LANES - N_EXPERTS)).reshape(1, LANES)
    return pl.pallas_call(
        _route_kernel,
        grid=(n_tok // ts,),
        in_specs=[
            pl.BlockSpec((ts, D_MODEL), lambda i: (i, 0)),
            pl.BlockSpec((2, D_MODEL, LANES), lambda i: (0, 0, 0)),
            pl.BlockSpec((1, LANES), lambda i: (0, 0)),
        ],
        out_specs=[
            pl.BlockSpec((ROUTE_COLS, ts), lambda i: (0, i)),
            pl.BlockSpec((ts, ROUTE_COLS), lambda i: (i, 0)),
            pl.BlockSpec((N_EXPERTS, LANES), lambda i: (0, 0)),
        ],
        out_shape=[
            jax.ShapeDtypeStruct((ROUTE_COLS, n_tok), jnp.int32),
            jax.ShapeDtypeStruct((n_tok, ROUTE_COLS), F32),
            jax.ShapeDtypeStruct((N_EXPERTS, LANES), F32),
        ],
        scratch_shapes=[pltpu.VMEM((N_EXPERTS, LANES), F32)],
        compiler_params=pltpu.CompilerParams(
            dimension_semantics=("arbitrary",), vmem_limit_bytes=VMEM_LIMIT_BYTES),
        name="moe_route",
    )(h2d, wr, br)


def _row_copy(src_ref, src_row, dst_ref, dst_row, sem):
    return pltpu.make_async_copy(src_ref.at[pl.ds(src_row, 1), :], dst_ref.at[pl.ds(dst_row, 1), :], sem)


def _invert_kernel(pos1_ref, pos2_ref, default_hbm, code_ref, sem):
    n_tok = pos1_ref.shape[0]
    s = pl.program_id(0)
    toks_per_step = n_tok // INVERT_STEPS

    @pl.when(s == 0)
    def _():
        fill = pltpu.make_async_copy(default_hbm, code_ref, sem)
        fill.start()
        fill.wait()

    base = s * toks_per_step

    def scatter(j, carry):
        t = base + j
        code_ref[pos1_ref[t]] = t
        code_ref[pos2_ref[t]] = n_tok + t
        return carry

    lax.fori_loop(0, toks_per_step, scatter, 0, unroll=8)


def _invert(pos1, pos2, n_code):
    n_tok = pos1.shape[0]
    assert n_tok % INVERT_STEPS == 0
    smem = pl.BlockSpec(memory_space=pltpu.SMEM)
    default = 2 * n_tok + jnp.arange(n_code, dtype=jnp.int32)
    return pl.pallas_call(
        _invert_kernel,
        grid=(INVERT_STEPS,),
        in_specs=[smem, smem, pl.BlockSpec(memory_space=pl.ANY)],
        out_specs=smem,
        out_shape=jax.ShapeDtypeStruct((n_code,), jnp.int32),
        scratch_shapes=[pltpu.SemaphoreType.DMA],
        compiler_params=pltpu.CompilerParams(dimension_semantics=("arbitrary",)),
        name="moe_invert",
    )(pos1, pos2, default)


def _ffn_moe_kernel(layer, te_ref, nu_ref, code_ref, h_hbm, wg_hbm, wu_hbm, wd_hbm, y_hbm,
                    xbuf, xb_s, a_s, obuf, wg_s, wu_s, wd_s, stage_in, stage_out, gsem, ssem, wsem):
    tm = MOE_TM
    n_tok = h_hbm.shape[0]
    d_ff = a_s.shape[1]
    n_chunks = d_ff // FF_CHUNK
    i = pl.program_id(0)
    n_used = nu_ref[0]

    def stream_cast(src_chunk, store_chunk, stage):
        copy = lambda c: pltpu.make_async_copy(src_chunk(c), stage.at[c % 2], wsem.at[c % 2])
        copy(0).start()
        for c in range(n_chunks):
            if c + 1 < n_chunks:
                copy(c + 1).start()
            copy(c).wait()
            store_chunk(c, stage[c % 2].astype(BF16))

    def load_expert(e):
        col = lambda c: pl.ds(c * FF_CHUNK, FF_CHUNK)

        def store_cols(dst):
            def store(c, val):
                dst[:, c * FF_CHUNK:(c + 1) * FF_CHUNK] = val
            return store

        def store_rows(c, val):
            wd_s[c * FF_CHUNK:(c + 1) * FF_CHUNK, :] = val

        stream_cast(lambda c: wg_hbm.at[layer, e, :, col(c)], store_cols(wg_s), stage_in)
        stream_cast(lambda c: wu_hbm.at[layer, e, :, col(c)], store_cols(wu_s), stage_in)
        stream_cast(lambda c: wd_hbm.at[layer, e, col(c), :], store_rows, stage_out)

    @pl.when((i < n_used) & ((i == 0) | (te_ref[i] != te_ref[jnp.maximum(i - 1, 0)])))
    def _():
        load_expert(te_ref[i])

    def gather_copy(r, code):
        tok = code & (n_tok - 1) if n_tok & (n_tok - 1) == 0 else lax.rem(code, n_tok)
        return _row_copy(h_hbm, tok, xbuf, r, gsem)

    def scatter_copy(r, code):
        return _row_copy(obuf, r, y_hbm, code, ssem)

    def issue_inline(make_copy, tile, part):
        n_parts = max(n_chunks - 2, 1)
        if part < n_parts:
            for r in range(part * tm // n_parts, (part + 1) * tm // n_parts):
                make_copy(r, code_ref[tile * tm + r]).start(priority=r % 2)

    def issue_loop(make_copy, tile):
        def body(r, carry):
            make_copy(r, code_ref[tile * tm + r]).start()
            return carry

        lax.fori_loop(0, tm, body, 0, unroll=8)

    def drain(make_copy):
        def body(r, carry):
            make_copy(r, 0).wait()
            return carry

        lax.fori_loop(0, tm, body, 0, unroll=8)

    def step(with_scatter):
        drain(gather_copy)
        xb_s[...] = xbuf[...].astype(BF16)

        def before_chunk(c):
            issue_inline(gather_copy, i + 1, c)
            if with_scatter:
                issue_inline(scatter_copy, i - 1, c)

        _swiglu_hidden(xb_s[...], wg_s, wu_s, a_s, before_chunk)
        if with_scatter:
            drain(scatter_copy)
        obuf[...] = _dot(a_s[...], wd_s[...])

    @pl.when(i == 0)
    def _():
        issue_loop(gather_copy, 0)
        step(False)

    @pl.when((i > 0) & (i < n_used))
    def _():
        step(True)

    @pl.when(i == n_used)
    def _():
        drain(gather_copy)
        issue_loop(scatter_copy, i - 1)
        drain(scatter_copy)


def _ffn_moe(h2d, tile_expert, n_used, code, n_y_rows, layer, wg_all, wu_all, wd_all):
    d_ff = wg_all.shape[3]
    tm = MOE_TM
    n_steps = tile_expert.shape[0]
    assert code.shape[0] == n_steps * tm and d_ff % FF_CHUNK == 0
    hbm = pl.BlockSpec(memory_space=pl.ANY)
    return pl.pallas_call(
        functools.partial(_ffn_moe_kernel, layer),
        grid_spec=pltpu.PrefetchScalarGridSpec(
            num_scalar_prefetch=3,
            grid=(n_steps,),
            in_specs=[hbm, hbm, hbm, hbm],
            out_specs=hbm,
            scratch_shapes=[
                pltpu.VMEM((tm, D_MODEL), F32),
                pltpu.VMEM((tm, D_MODEL), BF16),
                pltpu.VMEM((tm, d_ff), BF16),
                pltpu.VMEM((tm, D_MODEL), F32),
                pltpu.VMEM((D_MODEL, d_ff), BF16),
                pltpu.VMEM((D_MODEL, d_ff), BF16),
                pltpu.VMEM((d_ff, D_MODEL), BF16),
                pltpu.VMEM((2, D_MODEL, FF_CHUNK), F32),
                pltpu.VMEM((2, FF_CHUNK, D_MODEL), F32),
                pltpu.SemaphoreType.DMA,
                pltpu.SemaphoreType.DMA,
                pltpu.SemaphoreType.DMA((2,)),
            ],
        ),
        out_shape=jax.ShapeDtypeStruct((n_y_rows, D_MODEL), F32),
        compiler_params=pltpu.CompilerParams(
            dimension_semantics=("arbitrary",), vmem_limit_bytes=VMEM_LIMIT_BYTES),
        name="ffn_moe",
    )(tile_expert, n_used, code, h2d, wg_all, wu_all, wd_all)


COMBINE_TS = 512


def _combine_kernel(h_ref, wts_ref, y1_ref, y2_ref, lg_ref, lb_ref, o_ref):
    w = wts_ref[...]
    f = w[:, 0:1] * y1_ref[...] + w[:, 1:2] * y2_ref[...]
    o_ref[...] = _layer_norm_rows(ALPHA * h_ref[...] + f, lg_ref[...], lb_ref[...])


def _combine(h2d, wts, y, ln_g, ln_b):
    n_tok = h2d.shape[0]
    ts = COMBINE_TS
    assert n_tok % ts == 0
    row = lambda a: a.reshape(1, -1)
    second = n_tok // ts
    return pl.pallas_call(
        _combine_kernel,
        grid=(n_tok // ts,),
        in_specs=[
            pl.BlockSpec((ts, D_MODEL), lambda i: (i, 0)),
            pl.BlockSpec((ts, ROUTE_COLS), lambda i: (i, 0)),
            pl.BlockSpec((ts, D_MODEL), lambda i: (i, 0)),
            pl.BlockSpec((ts, D_MODEL), lambda i: (second + i, 0)),
            pl.BlockSpec((1, D_MODEL), lambda i: (0, 0)),
            pl.BlockSpec((1, D_MODEL), lambda i: (0, 0)),
        ],
        out_specs=pl.BlockSpec((ts, D_MODEL), lambda i: (i, 0)),
        out_shape=jax.ShapeDtypeStruct((n_tok, D_MODEL), F32),
        compiler_params=pltpu.CompilerParams(
            dimension_semantics=("arbitrary",), vmem_limit_bytes=VMEM_LIMIT_BYTES),
        name="moe_combine",
    )(h2d, wts, y, y, row(ln_g), row(ln_b))


def _moe_layer(h2d, w_router, b_router, layer, wg_all, wu_all, wd_all, ln_g, ln_b):
    n_tok = h2d.shape[0]
    tm = MOE_TM
    idx, wts, cnt = _route(h2d, w_router, b_router)

    counts = cnt[:, 0].astype(jnp.int32)
    padded = ((counts + tm - 1) // tm) * tm
    ends = jnp.cumsum(padded)
    starts = ends - padded
    n_steps = (TOP_K * n_tok) // tm + N_EXPERTS + 1
    n_steps += n_steps % 2
    n_used = (ends[-1] // tm).reshape(1)
    tile_id = jnp.minimum(jnp.arange(n_steps, dtype=jnp.int32), n_used - 1)
    tile_expert = jnp.minimum(
        jnp.sum((ends[None, :] <= (tile_id * tm)[:, None]).astype(jnp.int32), axis=1), N_EXPERTS - 1)
    start_of = lambda e: sum(jnp.where(e == k, starts[k], 0) for k in range(N_EXPERTS))
    pos1 = start_of(idx[0]) + idx[2]
    pos2 = start_of(idx[1]) + idx[3]

    n_code = n_steps * tm
    code = _invert(pos1, pos2, n_code)
    y = _ffn_moe(h2d, tile_expert, n_used, code, TOP_K * n_tok + n_code, layer, wg_all, wu_all, wd_all)
    return _combine(h2d, wts, y, ln_g, ln_b)


def kernel(x, w_in, b_gates, conv_w, conv_b, pool_w, pool_scale, mlstm_g, w_out, ln1_g, ln1_b, ln2_g, ln2_b, wg_dense, wu_dense, wd_dense, w_router, b_router, wg_exp, wu_exp, wd_exp):
    bsz, seq, d = x.shape
    x2d = x.reshape(bsz * seq, d)
    for l in range(DEPTH):
        h2d = _mixer_layer(x2d, bsz, seq, w_in[l], b_gates[l], conv_w[l], conv_b[l], pool_w[l],
                           pool_scale[l], mlstm_g[l], w_out[l], ln1_g[l], ln1_b[l])
        j = l // 2
        if l % 2 == 0:
            x2d = _ffn_dense_layer(h2d, wg_dense[j], wu_dense[j], wd_dense[j], ln2_g[l], ln2_b[l])
        else:
            x2d = _moe_layer(h2d, w_router[j], b_router[j], j, wg_exp, wu_exp, wd_exp,
                             ln2_g[l], ln2_b[l])
    return x2d.reshape(bsz, seq, d)
```

```python
import functools

import jax
import jax.numpy as jnp
from jax import lax
from jax.experimental import pallas as pl
from jax.experimental.pallas import tpu as pltpu

F32 = jnp.float32
BF16 = jnp.bfloat16

D_MODEL = 1024
DEPTH = 4
D_POOL = 512
POOL_WINDOWS = (2, 4, 8, 16)
POOL_GROUP = 128
MAX_WINDOW = max(POOL_WINDOWS)
D_MLSTM = 512
N_HEADS = 4
HEAD_DIM = 128
CONV_WIDTH = 4
N_GATES = 2 * N_HEADS
D_IN = D_POOL + 4 * D_MLSTM + N_GATES
N_EXPERTS = 8
TOP_K = 2
ALPHA = (2 * DEPTH) ** 0.25
LN_EPS = 1e-5

LANES = 128
SUBLANES = 8
D_IN_PAD = D_POOL + 4 * D_MLSTM + LANES
VMEM_LIMIT_BYTES = 56 * 1024 * 1024

OFF_QK = D_POOL
OFF_V = OFF_QK + 2 * D_MLSTM
OFF_O = OFF_V + D_MLSTM
OFF_G = OFF_O + D_MLSTM

MIX_TS = 1024
MIX_L = 256
CONV_PAD = SUBLANES
POOL_PAD = 16


def _dot(a, b):
    return jnp.dot(a, b, preferred_element_type=F32)


def _sigmoid(x):
    return 1.0 / (1.0 + jnp.exp(-x))


def _layer_norm_rows(r, g, b):
    mu = jnp.mean(r, axis=-1, keepdims=True)
    rc = r - mu
    var = jnp.mean(rc * rc, axis=-1, keepdims=True)
    return rc * lax.rsqrt(var + LN_EPS) * g + b


def _mixer_kernel(x_ref, win_ref, bg_ref, cw_ref, cb_ref, pw_ref, ps_ref, mg_ref, wout_ref,
                  lg_ref, lb_ref, h_ref,
                  pbuf, cbuf, q_s, k_s, ycat, st_s, m_s):
    ts, L = MIX_TS, MIX_L
    s_idx = pl.program_id(1)

    @pl.when(s_idx == 0)
    def _():
        pbuf[0:POOL_PAD, :] = jnp.zeros((POOL_PAD, D_POOL), F32)
        cbuf[0:CONV_PAD, :] = jnp.zeros((CONV_PAD, 2 * D_MLSTM), F32)
        st_s[...] = jnp.zeros_like(st_s)
        m_s[...] = jnp.zeros_like(m_s)

    xb = x_ref[...].astype(BF16)
    gates = _dot(xb, win_ref[:, OFF_G:D_IN_PAD]) + bg_ref[...]

    g_row = gates.T[0:N_GATES, :]
    logf_row = jnp.minimum(g_row, 0.0) - jnp.log1p(jnp.exp(-jnp.abs(g_row)))
    lane_in_chunk = lax.broadcasted_iota(jnp.int32, (N_GATES, ts), 1) % L
    b_row = logf_row
    sh = 1
    while sh < L:
        b_row = b_row + jnp.where(lane_in_chunk >= sh, pltpu.roll(b_row, sh, axis=1), 0.0)
        sh *= 2
    b_heads = b_row[N_HEADS:N_GATES, :]
    a_row = g_row[0:N_HEADS, :] - b_heads
    cmax_row = a_row
    sh = 1
    while sh < L:
        cmax_row = jnp.maximum(cmax_row, jnp.where(lane_in_chunk[0:N_HEADS] >= sh,
                                                   pltpu.roll(cmax_row, sh, axis=1), -jnp.inf))
        sh *= 2
    cols = jnp.concatenate([cmax_row, b_heads, jnp.zeros((LANES - N_GATES, ts), F32)], axis=0).T

    cbuf[CONV_PAD:CONV_PAD + ts, :] = _dot(xb, win_ref[:, OFF_QK:OFF_V])
    pbuf[POOL_PAD:POOL_PAD + ts, :] = _dot(xb, win_ref[:, 0:OFF_QK])
    v = _dot(xb, win_ref[:, OFF_V:OFF_O]).astype(BF16)
    o_pre = _dot(xb, win_ref[:, OFF_O:OFF_G])

    for c in range(2 * D_MLSTM // LANES):
        cs = slice(c * LANES, (c + 1) * LANES)
        y = cb_ref[:, cs] + cbuf[CONV_PAD:CONV_PAD + ts, cs] * cw_ref[CONV_WIDTH - 1:CONV_WIDTH, cs]
        for j in range(1, CONV_WIDTH):
            y = y + cbuf[CONV_PAD - j:CONV_PAD - j + ts, cs] * cw_ref[CONV_WIDTH - 1 - j:CONV_WIDTH - j, cs]
        y = y * _sigmoid(y)
        if c < N_HEADS:
            q_s[:, cs] = y.astype(BF16)
        else:
            ks = slice((c - N_HEADS) * LANES, (c - N_HEADS + 1) * LANES)
            k_s[:, ks] = (y * (HEAD_DIM ** -0.5)).astype(BF16)
    cbuf[0:CONV_PAD, :] = cbuf[ts:ts + CONV_PAD, :]

    t_glob = lax.broadcasted_iota(jnp.int32, (ts, 1), 0) + s_idx * ts
    for g, w in enumerate(POOL_WINDOWS):
        cs = slice(g * POOL_GROUP, (g + 1) * POOL_GROUP)
        acc = pbuf[:, cs]
        span = 1
        while span < w:
            acc = acc + pltpu.roll(acc, span, axis=0)
            span *= 2
        acc = acc[POOL_PAD:POOL_PAD + ts, :]
        u = pbuf[POOL_PAD:POOL_PAD + ts, cs]
        cnt = jnp.minimum(t_glob + 1, w).astype(F32)
        d = acc / cnt - u
        y = _dot(d.astype(BF16), pw_ref[g]) * ps_ref[:, cs]
        ycat[:, cs] = y.astype(BF16)
    pbuf[0:POOL_PAD, :] = pbuf[ts:ts + POOL_PAD, :]

    tri = (lax.broadcasted_iota(jnp.int32, (L, L), 1) <= lax.broadcasted_iota(jnp.int32, (L, L), 0))
    dn_nt = (((1,), (1,)), ((), ()))
    dn_tn = (((0,), (0,)), ((), ()))
    ones_blk = jnp.ones((L, HEAD_DIM), BF16)

    state = [st_s[h] for h in range(N_HEADS)]
    m_state = [m_s[h:h + 1, 0:1] for h in range(N_HEADS)]
    for c in range(ts // L):
        rs = slice(c * L, (c + 1) * L)
        for h in range(N_HEADS):
            hs = slice(h * HEAD_DIM, (h + 1) * HEAD_DIM)
            cm_c = cols[rs, h:h + 1]
            bc = cols[rs, N_HEADS + h:N_HEADS + h + 1]
            ic = gates[rs, h:h + 1]
            ar = a_row[h:h + 1, rs]
            m_prev = m_state[h]
            q_h = q_s[rs, hs]
            k_h = k_s[rs, hs]
            v2 = jnp.concatenate([v[rs, hs], ones_blk], axis=1)

            mm = jnp.maximum(cm_c, m_prev)
            s_qk = lax.dot_general(q_h, k_h, dn_nt, preferred_element_type=F32)
            wmat = jnp.exp(jnp.where(tri, ar - mm, -jnp.inf)) * s_qk
            s_inter = jnp.exp(m_prev - mm)
            tot = _dot(wmat.astype(BF16), v2) + s_inter * _dot(q_h, state[h].astype(BF16))
            num = tot[:, 0:HEAD_DIM]
            den = tot[:, HEAD_DIM:2 * HEAD_DIM]
            hh = num / jnp.maximum(jnp.abs(den), jnp.exp(-(bc + mm)))

            mu = jnp.mean(hh, axis=1, keepdims=True)
            hc = hh - mu
            var = jnp.mean(hc * hc, axis=1, keepdims=True)
            hn = hc * lax.rsqrt(var + LN_EPS) * mg_ref[:, hs]
            ym = _sigmoid(o_pre[rs, hs]) * hn
            ycat[rs, D_POOL + h * HEAD_DIM:D_POOL + (h + 1) * HEAD_DIM] = ym.astype(BF16)

            g_tot = bc[L - 1:L, :]
            m_loc = g_tot + cm_c[L - 1:L, :]
            wa = jnp.exp(g_tot + ic - bc - m_loc)
            wv2 = (wa * v2.astype(F32)).astype(BF16)
            st_loc = lax.dot_general(k_h, wv2, dn_tn, preferred_element_type=F32)
            m_new = jnp.maximum(g_tot + m_prev, m_loc)
            state[h] = jnp.exp(g_tot + m_prev - m_new) * state[h] + jnp.exp(m_loc - m_new) * st_loc
            m_state[h] = m_new
        mix = _dot(ycat[rs, :], wout_ref[...])
        h_ref[rs, :] = _layer_norm_rows(ALPHA * x_ref[rs, :] + mix, lg_ref[...], lb_ref[...])
    for h in range(N_HEADS):
        st_s[h] = state[h]
        m_s[h:h + 1, :] = jnp.broadcast_to(m_state[h], (1, LANES))


def _mixer_layer(x2d, bsz, seq, w_in, b_gates, conv_w, conv_b, pool_w, pool_scale, mlstm_g, w_out,
                 ln_g, ln_b):
    ts = MIX_TS
    assert seq % ts == 0 and ts % MIX_L == 0
    n_s_tiles = seq // ts
    w_in_p = jnp.pad(w_in, ((0, 0), (0, D_IN_PAD - D_IN))).astype(BF16)
    bg_p = jnp.pad(b_gates, (0, LANES - N_GATES)).reshape(1, LANES)
    row = lambda a: a.reshape(1, -1)
    const2 = lambda b, s: (0, 0)
    const3 = lambda b, s: (0, 0, 0)
    return pl.pallas_call(
        _mixer_kernel,
        grid=(bsz, n_s_tiles),
        in_specs=[
            pl.BlockSpec((ts, D_MODEL), lambda b, s: (b * n_s_tiles + s, 0)),
            pl.BlockSpec((D_MODEL, D_IN_PAD), const2),
            pl.BlockSpec((1, LANES), const2),
            pl.BlockSpec((CONV_WIDTH, 2 * D_MLSTM), const2),
            pl.BlockSpec((1, 2 * D_MLSTM), const2),
            pl.BlockSpec((len(POOL_WINDOWS), POOL_GROUP, POOL_GROUP), const3),
            pl.BlockSpec((1, D_POOL), const2),
            pl.BlockSpec((1, D_MLSTM), const2),
            pl.BlockSpec((D_MODEL, D_MODEL), const2),
            pl.BlockSpec((1, D_MODEL), const2),
            pl.BlockSpec((1, D_MODEL), const2),
        ],
        out_specs=pl.BlockSpec((ts, D_MODEL), lambda b, s: (b * n_s_tiles + s, 0)),
        out_shape=jax.ShapeDtypeStruct((bsz * seq, D_MODEL), F32),
        scratch_shapes=[
            pltpu.VMEM((POOL_PAD + ts, D_POOL), F32),
            pltpu.VMEM((CONV_PAD + ts, 2 * D_MLSTM), F32),
            pltpu.VMEM((ts, D_MLSTM), BF16),
            pltpu.VMEM((ts, D_MLSTM), BF16),
            pltpu.VMEM((ts, D_MODEL), BF16),
            pltpu.VMEM((N_HEADS, HEAD_DIM, 2 * HEAD_DIM), F32),
            pltpu.VMEM((SUBLANES, LANES), F32),
        ],
        compiler_params=pltpu.CompilerParams(
            dimension_semantics=("arbitrary", "arbitrary"),
            vmem_limit_bytes=VMEM_LIMIT_BYTES),
        name="mixer",
    )(x2d, w_in_p, bg_p, conv_w, row(conv_b), pool_w.astype(BF16), row(pool_scale), row(mlstm_g),
      w_out.astype(BF16), row(ln_g), row(ln_b))


DENSE_TM = 1024
FF_CHUNK = 512


def _swiglu_hidden(xb, wg, wu, a_s, before_chunk=None):
    d_ff = a_s.shape[1]
    for c, c0 in enumerate(range(0, d_ff, FF_CHUNK)):
        if before_chunk is not None:
            before_chunk(c)
        cs = slice(c0, min(c0 + FF_CHUNK, d_ff))
        g = _dot(xb, wg[:, cs])
        u = _dot(xb, wu[:, cs])
        a_s[:, cs] = (g * _sigmoid(g) * u).astype(BF16)


def _ffn_dense_kernel(x_ref, wg_ref, wu_ref, wd_ref, lg_ref, lb_ref, o_ref, a_s):
    x = x_ref[...]
    _swiglu_hidden(x.astype(BF16), wg_ref, wu_ref, a_s)
    f = _dot(a_s[...], wd_ref[...])
    o_ref[...] = _layer_norm_rows(ALPHA * x + f, lg_ref[...], lb_ref[...])


def _ffn_dense_layer(h2d, wg, wu, wd, ln_g, ln_b):
    n_tok = h2d.shape[0]
    d_ff = wg.shape[1]
    tm = DENSE_TM
    assert n_tok % tm == 0
    row = lambda a: a.reshape(1, -1)
    resident = dict(pipeline_mode=pl.Buffered(1))
    return pl.pallas_call(
        _ffn_dense_kernel,
        grid=(n_tok // tm,),
        in_specs=[
            pl.BlockSpec((tm, D_MODEL), lambda i: (i, 0)),
            pl.BlockSpec((D_MODEL, d_ff), lambda i: (0, 0), **resident),
            pl.BlockSpec((D_MODEL, d_ff), lambda i: (0, 0), **resident),
            pl.BlockSpec((d_ff, D_MODEL), lambda i: (0, 0), **resident),
            pl.BlockSpec((1, D_MODEL), lambda i: (0, 0)),
            pl.BlockSpec((1, D_MODEL), lambda i: (0, 0)),
        ],
        out_specs=pl.BlockSpec((tm, D_MODEL), lambda i: (i, 0)),
        out_shape=jax.ShapeDtypeStruct((n_tok, D_MODEL), F32),
        scratch_shapes=[pltpu.VMEM((tm, d_ff), BF16)],
        compiler_params=pltpu.CompilerParams(
            dimension_semantics=("arbitrary",), vmem_limit_bytes=VMEM_LIMIT_BYTES),
        name="ffn_dense",
    )(h2d, wg.astype(BF16), wu.astype(BF16), wd.astype(BF16), row(ln_g), row(ln_b))


ROUTE_TS = 1024
MOE_TM = 512
ROUTE_COLS = 8
INVERT_STEPS = 16


def _route_kernel(h_ref, wr_ref, br_ref, idx_ref, wts_ref, cnt_ref, carry_s):
    ts = ROUTE_TS
    i = pl.program_id(0)

    @pl.when(i == 0)
    def _():
        carry_s[...] = jnp.zeros_like(carry_s)

    h = h_ref[...]
    h_hi = h.astype(BF16)
    h_lo = (h - h_hi.astype(F32)).astype(BF16)
    logits = (_dot(h_hi, wr_ref[0]) + _dot(h_lo, wr_ref[0]) + _dot(h_hi, wr_ref[1])) + br_ref[...]
    lg = logits.T[0:N_EXPERTS, :]
    expert = lax.broadcasted_iota(jnp.int32, (N_EXPERTS, ts), 0)
    m1 = jnp.max(lg, axis=0, keepdims=True)
    i1 = jnp.min(jnp.where(lg == m1, expert, N_EXPERTS), axis=0, keepdims=True)
    oh1 = expert == i1
    lg2 = jnp.where(oh1, -jnp.inf, lg)
    m2 = jnp.max(lg2, axis=0, keepdims=True)
    i2 = jnp.min(jnp.where(lg2 == m2, expert, N_EXPERTS), axis=0, keepdims=True)
    oh2 = expert == i2
    e2 = jnp.exp(m2 - m1)
    w1 = 1.0 / (1.0 + e2)
    w2 = e2 / (1.0 + e2)

    sel = jnp.where(oh1 | oh2, 1.0, 0.0)
    strict_upper = jnp.where(lax.broadcasted_iota(jnp.int32, (ts, ts), 0)
                             < lax.broadcasted_iota(jnp.int32, (ts, ts), 1), 1.0, 0.0).astype(BF16)
    before = _dot(sel.astype(BF16), strict_upper) + carry_s[:, 0:1]
    r1 = jnp.sum(jnp.where(oh1, before, 0.0), axis=0, keepdims=True).astype(jnp.int32)
    r2 = jnp.sum(jnp.where(oh2, before, 0.0), axis=0, keepdims=True).astype(jnp.int32)
    carry_s[...] += jnp.sum(sel, axis=1, keepdims=True)

    idx_ref[...] = jnp.concatenate(
        [i1, i2, r1, r2, jnp.zeros((ROUTE_COLS - 4, ts), jnp.int32)], axis=0)
    w_rows = jnp.concatenate([w1, w2, jnp.zeros((LANES - 2, ts), F32)], axis=0)
    wts_ref[...] = w_rows.T[:, 0:ROUTE_COLS]
    cnt_ref[...] = carry_s[...]


def _route(h2d, w_router, b_router):
    n_tok = h2d.shape[0]
    ts = ROUTE_TS
    assert n_tok % ts == 0
    wr = jnp.pad(w_router, ((0, 0), (0, LANES - N_EXPERTS)))
    wr_hi = wr.astype(BF16)
    wr = jnp.stack([wr_hi, (wr - wr_hi.astype(F32)).astype(BF16)])
    br = jnp.pad(b_router, (0, LANES - N_EXPERTS)).reshape(1, LANES)
    return pl.pallas_call(
        _route_kernel,
        grid=(n_tok // ts,),
        in_specs=[
            pl.BlockSpec((ts, D_MODEL), lambda i: (i, 0)),
            pl.BlockSpec((2, D_MODEL, LANES), lambda i: (0, 0, 0)),
            pl.BlockSpec((1, LANES), lambda i: (0, 0)),
        ],
        out_specs=[
            pl.BlockSpec((ROUTE_COLS, ts), lambda i: (0, i)),
            pl.BlockSpec((ts, ROUTE_COLS), lambda i: (i, 0)),
            pl.BlockSpec((N_EXPERTS, LANES), lambda i: (0, 0)),
        ],
        out_shape=[
            jax.ShapeDtypeStruct((ROUTE_COLS, n_tok), jnp.int32),
            jax.ShapeDtypeStruct((n_tok, ROUTE_COLS), F32),
            jax.ShapeDtypeStruct((N_EXPERTS, LANES), F32),
        ],
        scratch_shapes=[pltpu.VMEM((N_EXPERTS, LANES), F32)],
        compiler_params=pltpu.CompilerParams(
            dimension_semantics=("arbitrary",), vmem_limit_bytes=VMEM_LIMIT_BYTES),
        name="moe_route",
    )(h2d, wr, br)


def _row_copy(src_ref, src_row, dst_ref, dst_row, sem):
    return pltpu.make_async_copy(src_ref.at[pl.ds(src_row, 1), :], dst_ref.at[pl.ds(dst_row, 1), :], sem)


def _invert_kernel(pos1_ref, pos2_ref, default_hbm, code_ref, sem):
    n_tok = pos1_ref.shape[0]
    s = pl.program_id(0)
    toks_per_step = n_tok // INVERT_STEPS

    @pl.when(s == 0)
    def _():
        fill = pltpu.make_async_copy(default_hbm, code_ref, sem)
        fill.start()
        fill.wait()

    base = s * toks_per_step

    def scatter(j, carry):
        t = base + j
        code_ref[pos1_ref[t]] = t
        code_ref[pos2_ref[t]] = n_tok + t
        return carry

    lax.fori_loop(0, toks_per_step, scatter, 0, unroll=8)


def _invert(pos1, pos2, n_code):
    n_tok = pos1.shape[0]
    assert n_tok % INVERT_STEPS == 0
    smem = pl.BlockSpec(memory_space=pltpu.SMEM)
    default = 2 * n_tok + jnp.arange(n_code, dtype=jnp.int32)
    return pl.pallas_call(
        _invert_kernel,
        grid=(INVERT_STEPS,),
        in_specs=[smem, smem, pl.BlockSpec(memory_space=pl.ANY)],
        out_specs=smem,
        out_shape=jax.ShapeDtypeStruct((n_code,), jnp.int32),
        scratch_shapes=[pltpu.SemaphoreType.DMA],
        compiler_params=pltpu.CompilerParams(dimension_semantics=("arbitrary",)),
        name="moe_invert",
    )(pos1, pos2, default)


def _ffn_moe_kernel(layer, te_ref, nu_ref, code_ref, h_hbm, wg_hbm, wu_hbm, wd_hbm, y_hbm,
                    xbuf, xb_s, a_s, obuf, wg_s, wu_s, wd_s, stage_in, stage_out, gsem, ssem, wsem):
    tm = MOE_TM
    n_tok = h_hbm.shape[0]
    d_ff = a_s.shape[1]
    n_chunks = d_ff // FF_CHUNK
    i = pl.program_id(0)
    n_used = nu_ref[0]

    def stream_cast(src_chunk, store_chunk, stage):
        copy = lambda c: pltpu.make_async_copy(src_chunk(c), stage.at[c % 2], wsem.at[c % 2])
        copy(0).start()
        for c in range(n_chunks):
            if c + 1 < n_chunks:
                copy(c + 1).start()
            copy(c).wait()
            store_chunk(c, stage[c % 2].astype(BF16))

    def load_expert(e):
        col = lambda c: pl.ds(c * FF_CHUNK, FF_CHUNK)

        def store_cols(dst):
            def store(c, val):
                dst[:, c * FF_CHUNK:(c + 1) * FF_CHUNK] = val
            return store

        def store_rows(c, val):
            wd_s[c * FF_CHUNK:(c + 1) * FF_CHUNK, :] = val

        stream_cast(lambda c: wg_hbm.at[layer, e, :, col(c)], store_cols(wg_s), stage_in)
        stream_cast(lambda c: wu_hbm.at[layer, e, :, col(c)], store_cols(wu_s), stage_in)
        stream_cast(lambda c: wd_hbm.at[layer, e, col(c), :], store_rows, stage_out)

    @pl.when((i < n_used) & ((i == 0) | (te_ref[i] != te_ref[jnp.maximum(i - 1, 0)])))
    def _():
        load_expert(te_ref[i])

    def gather_copy(r, code):
        tok = code & (n_tok - 1) if n_tok & (n_tok - 1) == 0 else lax.rem(code, n_tok)
        return _row_copy(h_hbm, tok, xbuf, r, gsem)

    def scatter_copy(r, code):
        return _row_copy(obuf, r, y_hbm, code, ssem)

    def issue_inline(make_copy, tile, part, priority):
        n_parts = max(n_chunks - 2, 1)
        if part < n_parts:
            for r in range(part * tm // n_parts, (part + 1) * tm // n_parts):
                make_copy(r, code_ref[tile * tm + r]).start(priority=priority)

    def issue_loop(make_copy, tile):
        def body(r, carry):
            make_copy(r, code_ref[tile * tm + r]).start()
            return carry

        lax.fori_loop(0, tm, body, 0, unroll=8)

    def drain(make_copy):
        def body(r, carry):
            make_copy(r, 0).wait()
            return carry

        lax.fori_loop(0, tm, body, 0, unroll=8)

    def step(with_scatter):
        drain(gather_copy)
        xb_s[...] = xbuf[...].astype(BF16)

        def before_chunk(c):
            issue_inline(gather_copy, i + 1, c, 0)
            if with_scatter:
                issue_inline(scatter_copy, i - 1, c, 1)

        _swiglu_hidden(xb_s[...], wg_s, wu_s, a_s, before_chunk)
        if with_scatter:
            drain(scatter_copy)
        obuf[...] = _dot(a_s[...], wd_s[...])

    @pl.when(i == 0)
    def _():
        issue_loop(gather_copy, 0)
        step(False)

    @pl.when((i > 0) & (i < n_used))
    def _():
        step(True)

    @pl.when(i == n_used)
    def _():
        drain(gather_copy)
        issue_loop(scatter_copy, i - 1)
        drain(scatter_copy)


def _ffn_moe(h2d, tile_expert, n_used, code, n_y_rows, layer, wg_all, wu_all, wd_all):
    d_ff = wg_all.shape[3]
    tm = MOE_TM
    n_steps = tile_expert.shape[0]
    assert code.shape[0] == n_steps * tm and d_ff % FF_CHUNK == 0
    hbm = pl.BlockSpec(memory_space=pl.ANY)
    return pl.pallas_call(
        functools.partial(_ffn_moe_kernel, layer),
        grid_spec=pltpu.PrefetchScalarGridSpec(
            num_scalar_prefetch=3,
            grid=(n_steps,),
            in_specs=[hbm, hbm, hbm, hbm],
            out_specs=hbm,
            scratch_shapes=[
                pltpu.VMEM((tm, D_MODEL), F32),
                pltpu.VMEM((tm, D_MODEL), BF16),
                pltpu.VMEM((tm, d_ff), BF16),
                pltpu.VMEM((tm, D_MODEL), F32),
                pltpu.VMEM((D_MODEL, d_ff), BF16),
                pltpu.VMEM((D_MODEL, d_ff), BF16),
                pltpu.VMEM((d_ff, D_MODEL), BF16),
                pltpu.VMEM((2, D_MODEL, FF_CHUNK), F32),
                pltpu.VMEM((2, FF_CHUNK, D_MODEL), F32),
                pltpu.SemaphoreType.DMA,
                pltpu.SemaphoreType.DMA,
                pltpu.SemaphoreType.DMA((2,)),
            ],
        ),
        out_shape=jax.ShapeDtypeStruct((n_y_rows, D_MODEL), F32),
        compiler_params=pltpu.CompilerParams(
            dimension_semantics=("arbitrary",), vmem_limit_bytes=VMEM_LIMIT_BYTES),
        name="ffn_moe",
    )(tile_expert, n_used, code, h2d, wg_all, wu_all, wd_all)


COMBINE_TS = 1024


def _combine_kernel(h_ref, wts_ref, y1_ref, y2_ref, lg_ref, lb_ref, o_ref):
    w = wts_ref[...]
    f = w[:, 0:1] * y1_ref[...] + w[:, 1:2] * y2_ref[...]
    o_ref[...] = _layer_norm_rows(ALPHA * h_ref[...] + f, lg_ref[...], lb_ref[...])


def _combine(h2d, wts, y, ln_g, ln_b):
    n_tok = h2d.shape[0]
    ts = COMBINE_TS
    assert n_tok % ts == 0
    row = lambda a: a.reshape(1, -1)
    second = n_tok // ts
    return pl.pallas_call(
        _combine_kernel,
        grid=(n_tok // ts,),
        in_specs=[
            pl.BlockSpec((ts, D_MODEL), lambda i: (i, 0)),
            pl.BlockSpec((ts, ROUTE_COLS), lambda i: (i, 0)),
            pl.BlockSpec((ts, D_MODEL), lambda i: (i, 0)),
            pl.BlockSpec((ts, D_MODEL), lambda i: (second + i, 0)),
            pl.BlockSpec((1, D_MODEL), lambda i: (0, 0)),
            pl.BlockSpec((1, D_MODEL), lambda i: (0, 0)),
        ],
        out_specs=pl.BlockSpec((ts, D_MODEL), lambda i: (i, 0)),
        out_shape=jax.ShapeDtypeStruct((n_tok, D_MODEL), F32),
        compiler_params=pltpu.CompilerParams(
            dimension_semantics=("arbitrary",), vmem_limit_bytes=VMEM_LIMIT_BYTES),
        name="moe_combine",
    )(h2d, wts, y, y, row(ln_g), row(ln_b))


def _moe_layer(h2d, w_router, b_router, layer, wg_all, wu_all, wd_all, ln_g, ln_b):
    n_tok = h2d.shape[0]
    tm = MOE_TM
    idx, wts, cnt = _route(h2d, w_router, b_router)

    counts = cnt[:, 0].astype(jnp.int32)
    padded = ((counts + tm - 1) // tm) * tm
    ends = jnp.cumsum(padded)
    starts = ends - padded
    n_steps = (TOP_K * n_tok) // tm + N_EXPERTS + 1
    n_steps += n_steps % 2
    n_used = (ends[-1] // tm).reshape(1)
    tile_id = jnp.minimum(jnp.arange(n_steps, dtype=jnp.int32), n_used - 1)
    tile_expert = jnp.minimum(
        jnp.sum((ends[None, :] <= (tile_id * tm)[:, None]).astype(jnp.int32), axis=1), N_EXPERTS - 1)
    start_of = lambda e: sum(jnp.where(e == k, starts[k], 0) for k in range(N_EXPERTS))
    pos1 = start_of(idx[0]) + idx[2]
    pos2 = start_of(idx[1]) + idx[3]

    n_code = n_steps * tm
    code = _invert(pos1, pos2, n_code)
    y = _ffn_moe(h2d, tile_expert, n_used, code, TOP_K * n_tok + n_code, layer, wg_all, wu_all, wd_all)
    return _combine(h2d, wts, y, ln_g, ln_b)


def kernel(x, w_in, b_gates, conv_w, conv_b, pool_w, pool_scale, mlstm_g, w_out, ln1_g, ln1_b, ln2_g, ln2_b, wg_dense, wu_dense, wd_dense, w_router, b_router, wg_exp, wu_exp, wd_exp):
    bsz, seq, d = x.shape
    x2d = x.reshape(bsz * seq, d)
    for l in range(DEPTH):
        h2d = _mixer_layer(x2d, bsz, seq, w_in[l], b_gates[l], conv_w[l], conv_b[l], pool_w[l],
                           pool_scale[l], mlstm_g[l], w_out[l], ln1_g[l], ln1_b[l])
        j = l // 2
        if l % 2 == 0:
            x2d = _ffn_dense_layer(h2d, wg_dense[j], wu_dense[j], wd_dense[j], ln2_g[l], ln2_b[l])
        else:
            x2d = _moe_layer(h2d, w_router[j], b_router[j], j, wg_exp, wu_exp, wd_exp,
                             ln2_g[l], ln2_b[l])
    return x2d.reshape(bsz, seq, d)
```

```python
import functools

import jax
import jax.numpy as jnp
from jax import lax
from jax.experimental import pallas as pl
from jax.experimental.pallas import tpu as pltpu

F32 = jnp.float32
BF16 = jnp.bfloat16

D_MODEL = 1024
DEPTH = 4
D_POOL = 512
POOL_WINDOWS = (2, 4, 8, 16)
POOL_GROUP = 128
MAX_WINDOW = max(POOL_WINDOWS)
D_MLSTM = 512
N_HEADS = 4
HEAD_DIM = 128
CONV_WIDTH = 4
N_GATES = 2 * N_HEADS
D_IN = D_POOL + 4 * D_MLSTM + N_GATES
N_EXPERTS = 8
TOP_K = 2
ALPHA = (2 * DEPTH) ** 0.25
LN_EPS = 1e-5

LANES = 128
SUBLANES = 8
D_IN_PAD = D_POOL + 4 * D_MLSTM + LANES
VMEM_LIMIT_BYTES = 56 * 1024 * 1024

OFF_QK = D_POOL
OFF_V = OFF_QK + 2 * D_MLSTM
OFF_O = OFF_V + D_MLSTM
OFF_G = OFF_O + D_MLSTM

MIX_TS = 1024
MIX_L = 256
CONV_PAD = SUBLANES
POOL_PAD = 16


def _dot(a, b):
    return jnp.dot(a, b, preferred_element_type=F32)


def _sigmoid(x):
    return 1.0 / (1.0 + jnp.exp(-x))


def _layer_norm_rows(r, g, b):
    mu = jnp.mean(r, axis=-1, keepdims=True)
    rc = r - mu
    var = jnp.mean(rc * rc, axis=-1, keepdims=True)
    return rc * lax.rsqrt(var + LN_EPS) * g + b


def _mixer_kernel(x_ref, win_ref, bg_ref, cw_ref, cb_ref, pw_ref, ps_ref, mg_ref, wout_ref,
                  lg_ref, lb_ref, h_ref,
                  pbuf, cbuf, q_s, k_s, ycat, st_s, m_s):
    ts, L = MIX_TS, MIX_L
    s_idx = pl.program_id(1)

    @pl.when(s_idx == 0)
    def _():
        pbuf[0:POOL_PAD, :] = jnp.zeros((POOL_PAD, D_POOL), F32)
        cbuf[0:CONV_PAD, :] = jnp.zeros((CONV_PAD, 2 * D_MLSTM), F32)
        st_s[...] = jnp.zeros_like(st_s)
        m_s[...] = jnp.zeros_like(m_s)

    xb = x_ref[...].astype(BF16)
    gates = _dot(xb, win_ref[:, OFF_G:D_IN_PAD]) + bg_ref[...]

    g_row = gates.T[0:N_GATES, :]
    logf_row = jnp.minimum(g_row, 0.0) - jnp.log1p(jnp.exp(-jnp.abs(g_row)))
    lane_in_chunk = lax.broadcasted_iota(jnp.int32, (N_GATES, ts), 1) % L
    b_row = logf_row
    sh = 1
    while sh < L:
        b_row = b_row + jnp.where(lane_in_chunk >= sh, pltpu.roll(b_row, sh, axis=1), 0.0)
        sh *= 2
    b_heads = b_row[N_HEADS:N_GATES, :]
    a_row = g_row[0:N_HEADS, :] - b_heads
    cmax_row = a_row
    sh = 1
    while sh < L:
        cmax_row = jnp.maximum(cmax_row, jnp.where(lane_in_chunk[0:N_HEADS] >= sh,
                                                   pltpu.roll(cmax_row, sh, axis=1), -jnp.inf))
        sh *= 2
    cols = jnp.concatenate([cmax_row, b_heads, jnp.zeros((LANES - N_GATES, ts), F32)], axis=0).T

    cbuf[CONV_PAD:CONV_PAD + ts, :] = _dot(xb, win_ref[:, OFF_QK:OFF_V])
    pbuf[POOL_PAD:POOL_PAD + ts, :] = _dot(xb, win_ref[:, 0:OFF_QK])
    v = _dot(xb, win_ref[:, OFF_V:OFF_O]).astype(BF16)
    o_pre = _dot(xb, win_ref[:, OFF_O:OFF_G])

    for c in range(2 * D_MLSTM // LANES):
        cs = slice(c * LANES, (c + 1) * LANES)
        y = cb_ref[:, cs] + cbuf[CONV_PAD:CONV_PAD + ts, cs] * cw_ref[CONV_WIDTH - 1:CONV_WIDTH, cs]
        for j in range(1, CONV_WIDTH):
            y = y + cbuf[CONV_PAD - j:CONV_PAD - j + ts, cs] * cw_ref[CONV_WIDTH - 1 - j:CONV_WIDTH - j, cs]
        y = y * _sigmoid(y)
        if c < N_HEADS:
            q_s[:, cs] = y.astype(BF16)
        else:
            ks = slice((c - N_HEADS) * LANES, (c - N_HEADS + 1) * LANES)
            k_s[:, ks] = (y * (HEAD_DIM ** -0.5)).astype(BF16)
    cbuf[0:CONV_PAD, :] = cbuf[ts:ts + CONV_PAD, :]

    t_glob = lax.broadcasted_iota(jnp.int32, (ts, 1), 0) + s_idx * ts
    for g, w in enumerate(POOL_WINDOWS):
        cs = slice(g * POOL_GROUP, (g + 1) * POOL_GROUP)
        acc = pbuf[:, cs]
        span = 1
        while span < w:
            acc = acc + pltpu.roll(acc, span, axis=0)
            span *= 2
        acc = acc[POOL_PAD:POOL_PAD + ts, :]
        u = pbuf[POOL_PAD:POOL_PAD + ts, cs]
        cnt = jnp.minimum(t_glob + 1, w).astype(F32)
        d = acc / cnt - u
        y = _dot(d.astype(BF16), pw_ref[g]) * ps_ref[:, cs]
        ycat[:, cs] = y.astype(BF16)
    pbuf[0:POOL_PAD, :] = pbuf[ts:ts + POOL_PAD, :]

    tri = (lax.broadcasted_iota(jnp.int32, (L, L), 1) <= lax.broadcasted_iota(jnp.int32, (L, L), 0))
    dn_nt = (((1,), (1,)), ((), ()))
    dn_tn = (((0,), (0,)), ((), ()))
    ones_blk = jnp.ones((L, HEAD_DIM), BF16)

    state = [st_s[h] for h in range(N_HEADS)]
    m_state = [m_s[h:h + 1, 0:1] for h in range(N_HEADS)]
    for c in range(ts // L):
        rs = slice(c * L, (c + 1) * L)
        for h in range(N_HEADS):
            hs = slice(h * HEAD_DIM, (h + 1) * HEAD_DIM)
            cm_c = cols[rs, h:h + 1]
            bc = cols[rs, N_HEADS + h:N_HEADS + h + 1]
            ic = gates[rs, h:h + 1]
            ar = a_row[h:h + 1, rs]
            m_prev = m_state[h]
            q_h = q_s[rs, hs]
            k_h = k_s[rs, hs]
            v2 = jnp.concatenate([v[rs, hs], ones_blk], axis=1)

            mm = jnp.maximum(cm_c, m_prev)
            s_qk = lax.dot_general(q_h, k_h, dn_nt, preferred_element_type=F32)
            wmat = jnp.exp(jnp.where(tri, ar - mm, -jnp.inf)) * s_qk
            s_inter = jnp.exp(m_prev - mm)
            tot = _dot(wmat.astype(BF16), v2) + s_inter * _dot(q_h, state[h].astype(BF16))
            num = tot[:, 0:HEAD_DIM]
            den = tot[:, HEAD_DIM:2 * HEAD_DIM]
            hh = num / jnp.maximum(jnp.abs(den), jnp.exp(-(bc + mm)))

            mu = jnp.mean(hh, axis=1, keepdims=True)
            hc = hh - mu
            var = jnp.mean(hc * hc, axis=1, keepdims=True)
            hn = hc * lax.rsqrt(var + LN_EPS) * mg_ref[:, hs]
            ym = _sigmoid(o_pre[rs, hs]) * hn
            ycat[rs, D_POOL + h * HEAD_DIM:D_POOL + (h + 1) * HEAD_DIM] = ym.astype(BF16)

            g_tot = bc[L - 1:L, :]
            m_loc = g_tot + cm_c[L - 1:L, :]
            wa = jnp.exp(g_tot + ic - bc - m_loc)
            wv2 = (wa * v2.astype(F32)).astype(BF16)
            st_loc = lax.dot_general(k_h, wv2, dn_tn, preferred_element_type=F32)
            m_new = jnp.maximum(g_tot + m_prev, m_loc)
            state[h] = jnp.exp(g_tot + m_prev - m_new) * state[h] + jnp.exp(m_loc - m_new) * st_loc
            m_state[h] = m_new
        mix = _dot(ycat[rs, :], wout_ref[...])
        h_ref[rs, :] = _layer_norm_rows(ALPHA * x_ref[rs, :] + mix, lg_ref[...], lb_ref[...])
    for h in range(N_HEADS):
        st_s[h] = state[h]
        m_s[h:h + 1, :] = jnp.broadcast_to(m_state[h], (1, LANES))


def _mixer_layer(x2d, bsz, seq, w_in, b_gates, conv_w, conv_b, pool_w, pool_scale, mlstm_g, w_out,
                 ln_g, ln_b):
    ts = MIX_TS
    assert seq % ts == 0 and ts % MIX_L == 0
    n_s_tiles = seq // ts
    w_in_p = jnp.pad(w_in, ((0, 0), (0, D_IN_PAD - D_IN))).astype(BF16)
    bg_p = jnp.pad(b_gates, (0, LANES - N_GATES)).reshape(1, LANES)
    row = lambda a: a.reshape(1, -1)
    const2 = lambda b, s: (0, 0)
    const3 = lambda b, s: (0, 0, 0)
    return pl.pallas_call(
        _mixer_kernel,
        grid=(bsz, n_s_tiles),
        in_specs=[
            pl.BlockSpec((ts, D_MODEL), lambda b, s: (b * n_s_tiles + s, 0)),
            pl.BlockSpec((D_MODEL, D_IN_PAD), const2),
            pl.BlockSpec((1, LANES), const2),
            pl.BlockSpec((CONV_WIDTH, 2 * D_MLSTM), const2),
            pl.BlockSpec((1, 2 * D_MLSTM), const2),
            pl.BlockSpec((len(POOL_WINDOWS), POOL_GROUP, POOL_GROUP), const3),
            pl.BlockSpec((1, D_POOL), const2),
            pl.BlockSpec((1, D_MLSTM), const2),
            pl.BlockSpec((D_MODEL, D_MODEL), const2),
            pl.BlockSpec((1, D_MODEL), const2),
            pl.BlockSpec((1, D_MODEL), const2),
        ],
        out_specs=pl.BlockSpec((ts, D_MODEL), lambda b, s: (b * n_s_tiles + s, 0)),
        out_shape=jax.ShapeDtypeStruct((bsz * seq, D_MODEL), F32),
        scratch_shapes=[
            pltpu.VMEM((POOL_PAD + ts, D_POOL), F32),
            pltpu.VMEM((CONV_PAD + ts, 2 * D_MLSTM), F32),
            pltpu.VMEM((ts, D_MLSTM), BF16),
            pltpu.VMEM((ts, D_MLSTM), BF16),
            pltpu.VMEM((ts, D_MODEL), BF16),
            pltpu.VMEM((N_HEADS, HEAD_DIM, 2 * HEAD_DIM), F32),
            pltpu.VMEM((SUBLANES, LANES), F32),
        ],
        compiler_params=pltpu.CompilerParams(
            dimension_semantics=("arbitrary", "arbitrary"),
            vmem_limit_bytes=VMEM_LIMIT_BYTES),
        name="mixer",
    )(x2d, w_in_p, bg_p, conv_w, row(conv_b), pool_w.astype(BF16), row(pool_scale), row(mlstm_g),
      w_out.astype(BF16), row(ln_g), row(ln_b))


DENSE_TM = 1024
FF_CHUNK = 512


def _swiglu_hidden(xb, wg, wu, a_s, before_chunk=None):
    d_ff = a_s.shape[1]
    for c, c0 in enumerate(range(0, d_ff, FF_CHUNK)):
        if before_chunk is not None:
            before_chunk(c)
        cs = slice(c0, min(c0 + FF_CHUNK, d_ff))
        g = _dot(xb, wg[:, cs])
        u = _dot(xb, wu[:, cs])
        a_s[:, cs] = (g * _sigmoid(g) * u).astype(BF16)


def _ffn_dense_kernel(x_ref, wg_ref, wu_ref, wd_ref, lg_ref, lb_ref, o_ref, a_s):
    x = x_ref[...]
    _swiglu_hidden(x.astype(BF16), wg_ref, wu_ref, a_s)
    f = _dot(a_s[...], wd_ref[...])
    o_ref[...] = _layer_norm_rows(ALPHA * x + f, lg_ref[...], lb_ref[...])


def _ffn_dense_layer(h2d, wg, wu, wd, ln_g, ln_b):
    n_tok = h2d.shape[0]
    d_ff = wg.shape[1]
    tm = DENSE_TM
    assert n_tok % tm == 0
    row = lambda a: a.reshape(1, -1)
    resident = dict(pipeline_mode=pl.Buffered(1))
    return pl.pallas_call(
        _ffn_dense_kernel,
        grid=(n_tok // tm,),
        in_specs=[
            pl.BlockSpec((tm, D_MODEL), lambda i: (i, 0)),
            pl.BlockSpec((D_MODEL, d_ff), lambda i: (0, 0), **resident),
            pl.BlockSpec((D_MODEL, d_ff), lambda i: (0, 0), **resident),
            pl.BlockSpec((d_ff, D_MODEL), lambda i: (0, 0), **resident),
            pl.BlockSpec((1, D_MODEL), lambda i: (0, 0)),
            pl.BlockSpec((1, D_MODEL), lambda i: (0, 0)),
        ],
        out_specs=pl.BlockSpec((tm, D_MODEL), lambda i: (i, 0)),
        out_shape=jax.ShapeDtypeStruct((n_tok, D_MODEL), F32),
        scratch_shapes=[pltpu.VMEM((tm, d_ff), BF16)],
        compiler_params=pltpu.CompilerParams(
            dimension_semantics=("arbitrary",), vmem_limit_bytes=VMEM_LIMIT_BYTES),
        name="ffn_dense",
    )(h2d, wg.astype(BF16), wu.astype(BF16), wd.astype(BF16), row(ln_g), row(ln_b))


ROUTE_TS = 1024
MOE_TM = 1024
ROUTE_COLS = 8
INVERT_STEPS = 16


def _route_kernel(h_ref, wr_ref, br_ref, idx_ref, wts_ref, cnt_ref, carry_s):
    ts = ROUTE_TS
    i = pl.program_id(0)

    @pl.when(i == 0)
    def _():
        carry_s[...] = jnp.zeros_like(carry_s)

    h = h_ref[...]
    h_hi = h.astype(BF16)
    h_lo = (h - h_hi.astype(F32)).astype(BF16)
    logits = (_dot(h_hi, wr_ref[0]) + _dot(h_lo, wr_ref[0]) + _dot(h_hi, wr_ref[1])) + br_ref[...]
    lg = logits.T[0:N_EXPERTS, :]
    expert = lax.broadcasted_iota(jnp.int32, (N_EXPERTS, ts), 0)
    m1 = jnp.max(lg, axis=0, keepdims=True)
    i1 = jnp.min(jnp.where(lg == m1, expert, N_EXPERTS), axis=0, keepdims=True)
    oh1 = expert == i1
    lg2 = jnp.where(oh1, -jnp.inf, lg)
    m2 = jnp.max(lg2, axis=0, keepdims=True)
    i2 = jnp.min(jnp.where(lg2 == m2, expert, N_EXPERTS), axis=0, keepdims=True)
    oh2 = expert == i2
    e2 = jnp.exp(m2 - m1)
    w1 = 1.0 / (1.0 + e2)
    w2 = e2 / (1.0 + e2)

    sel = jnp.where(oh1 | oh2, 1.0, 0.0)
    strict_upper = jnp.where(lax.broadcasted_iota(jnp.int32, (ts, ts), 0)
                             < lax.broadcasted_iota(jnp.int32, (ts, ts), 1), 1.0, 0.0).astype(BF16)
    before = _dot(sel.astype(BF16), strict_upper) + carry_s[:, 0:1]
    r1 = jnp.sum(jnp.where(oh1, before, 0.0), axis=0, keepdims=True).astype(jnp.int32)
    r2 = jnp.sum(jnp.where(oh2, before, 0.0), axis=0, keepdims=True).astype(jnp.int32)
    carry_s[...] += jnp.sum(sel, axis=1, keepdims=True)

    idx_ref[...] = jnp.concatenate(
        [i1, i2, r1, r2, jnp.zeros((ROUTE_COLS - 4, ts), jnp.int32)], axis=0)
    w_rows = jnp.concatenate([w1, w2, jnp.zeros((LANES - 2, ts), F32)], axis=0)
    wts_ref[...] = w_rows.T[:, 0:ROUTE_COLS]
    cnt_ref[...] = carry_s[...]


def _route(h2d, w_router, b_router):
    n_tok = h2d.shape[0]
    ts = ROUTE_TS
    assert n_tok % ts == 0
    wr = jnp.pad(w_router, ((0, 0), (0, LANES - N_EXPERTS)))
    wr_hi = wr.astype(BF16)
    wr = jnp.stack([wr_hi, (wr - wr_hi.astype(F32)).astype(BF16)])
    br = jnp.pad(b_router, (0, LANES - N_EXPERTS)).reshape(1, LANES)
    return pl.pallas_call(
        _route_kernel,
        grid=(n_tok // ts,),
        in_specs=[
            pl.BlockSpec((ts, D_MODEL), lambda i: (i, 0)),
            pl.BlockSpec((2, D_MODEL, LANES), lambda i: (0, 0, 0)),
            pl.BlockSpec((1, LANES), lambda i: (0, 0)),
        ],
        out_specs=[
            pl.BlockSpec((ROUTE_COLS, ts), lambda i: (0, i)),
            pl.BlockSpec((ts, ROUTE_COLS), lambda i: (i, 0)),
            pl.BlockSpec((N_EXPERTS, LANES), lambda i: (0, 0)),
        ],
        out_shape=[
            jax.ShapeDtypeStruct((ROUTE_COLS, n_tok), jnp.int32),
            jax.ShapeDtypeStruct((n_tok, ROUTE_COLS), F32),
            jax.ShapeDtypeStruct((N_EXPERTS, LANES), F32),
        ],
        scratch_shapes=[pltpu.VMEM((N_EXPERTS, LANES), F32)],
        compiler_params=pltpu.CompilerParams(
            dimension_semantics=("arbitrary",), vmem_limit_bytes=VMEM_LIMIT_BYTES),
        name="moe_route",
    )(h2d, wr, br)


def _row_copy(src_ref, src_row, dst_ref, dst_row, sem):
    return pltpu.make_async_copy(src_ref.at[pl.ds(src_row, 1), :], dst_ref.at[pl.ds(dst_row, 1), :], sem)


def _invert_kernel(pos1_ref, pos2_ref, default_hbm, code_ref, sem):
    n_tok = pos1_ref.shape[0]
    s = pl.program_id(0)
    toks_per_step = n_tok // INVERT_STEPS

    @pl.when(s == 0)
    def _():
        fill = pltpu.make_async_copy(default_hbm, code_ref, sem)
        fill.start()
        fill.wait()

    base = s * toks_per_step

    def scatter(j, carry):
        t = base + j
        code_ref[pos1_ref[t]] = t
        code_ref[pos2_ref[t]] = n_tok + t
        return carry

    lax.fori_loop(0, toks_per_step, scatter, 0, unroll=8)


def _invert(pos1, pos2, n_code):
    n_tok = pos1.shape[0]
    assert n_tok % INVERT_STEPS == 0
    smem = pl.BlockSpec(memory_space=pltpu.SMEM)
    default = 2 * n_tok + jnp.arange(n_code, dtype=jnp.int32)
    return pl.pallas_call(
        _invert_kernel,
        grid=(INVERT_STEPS,),
        in_specs=[smem, smem, pl.BlockSpec(memory_space=pl.ANY)],
        out_specs=smem,
        out_shape=jax.ShapeDtypeStruct((n_code,), jnp.int32),
        scratch_shapes=[pltpu.SemaphoreType.DMA],
        compiler_params=pltpu.CompilerParams(dimension_semantics=("arbitrary",)),
        name="moe_invert",
    )(pos1, pos2, default)


def _ffn_moe_kernel(layer, te_ref, nu_ref, code_ref, h_hbm, wg_hbm, wu_hbm, wd_hbm, y_hbm,
                    xbuf, xb_s, a_s, obuf, wg_s, wu_s, wd_s, stage_in, stage_out, gsem, ssem, wsem):
    tm = MOE_TM
    n_tok = h_hbm.shape[0]
    d_ff = a_s.shape[1]
    n_chunks = d_ff // FF_CHUNK
    i = pl.program_id(0)
    n_used = nu_ref[0]

    def stream_cast(src_chunk, store_chunk, stage):
        copy = lambda c: pltpu.make_async_copy(src_chunk(c), stage.at[c % 2], wsem.at[c % 2])
        copy(0).start()
        for c in range(n_chunks):
            if c + 1 < n_chunks:
                copy(c + 1).start()
            copy(c).wait()
            store_chunk(c, stage[c % 2].astype(BF16))

    def load_expert(e):
        col = lambda c: pl.ds(c * FF_CHUNK, FF_CHUNK)

        def store_cols(dst):
            def store(c, val):
                dst[:, c * FF_CHUNK:(c + 1) * FF_CHUNK] = val
            return store

        def store_rows(c, val):
            wd_s[c * FF_CHUNK:(c + 1) * FF_CHUNK, :] = val

        stream_cast(lambda c: wg_hbm.at[layer, e, :, col(c)], store_cols(wg_s), stage_in)
        stream_cast(lambda c: wu_hbm.at[layer, e, :, col(c)], store_cols(wu_s), stage_in)
        stream_cast(lambda c: wd_hbm.at[layer, e, col(c), :], store_rows, stage_out)

    @pl.when((i < n_used) & ((i == 0) | (te_ref[i] != te_ref[jnp.maximum(i - 1, 0)])))
    def _():
        load_expert(te_ref[i])

    def gather_copy(r, code):
        tok = code & (n_tok - 1) if n_tok & (n_tok - 1) == 0 else lax.rem(code, n_tok)
        return _row_copy(h_hbm, tok, xbuf, r, gsem)

    def scatter_copy(r, code):
        return _row_copy(obuf, r, y_hbm, code, ssem)

    def issue_inline(make_copy, tile, part, priority):
        n_parts = max(n_chunks - 2, 1)
        if part < n_parts:
            for r in range(part * tm // n_parts, (part + 1) * tm // n_parts):
                make_copy(r, code_ref[tile * tm + r]).start(priority=priority)

    def issue_loop(make_copy, tile):
        def body(r, carry):
            make_copy(r, code_ref[tile * tm + r]).start()
            return carry

        lax.fori_loop(0, tm, body, 0, unroll=8)

    def drain(make_copy):
        def body(r, carry):
            make_copy(r, 0).wait()
            return carry

        lax.fori_loop(0, tm, body, 0, unroll=8)

    def step(with_scatter):
        drain(gather_copy)
        xb_s[...] = xbuf[...].astype(BF16)

        def before_chunk(c):
            issue_inline(gather_copy, i + 1, c, 0)
            if with_scatter:
                issue_inline(scatter_copy, i - 1, c, 1)

        _swiglu_hidden(xb_s[...], wg_s, wu_s, a_s, before_chunk)
        if with_scatter:
            drain(scatter_copy)
        obuf[...] = _dot(a_s[...], wd_s[...])

    @pl.when(i == 0)
    def _():
        issue_loop(gather_copy, 0)
        step(False)

    @pl.when((i > 0) & (i < n_used))
    def _():
        step(True)

    @pl.when(i == n_used)
    def _():
        drain(gather_copy)
        issue_loop(scatter_copy, i - 1)
        drain(scatter_copy)


def _ffn_moe(h2d, tile_expert, n_used, code, n_y_rows, layer, wg_all, wu_all, wd_all):
    d_ff = wg_all.shape[3]
    tm = MOE_TM
    n_steps = tile_expert.shape[0]
    assert code.shape[0] == n_steps * tm and d_ff % FF_CHUNK == 0
    hbm = pl.BlockSpec(memory_space=pl.ANY)
    return pl.pallas_call(
        functools.partial(_ffn_moe_kernel, layer),
        grid_spec=pltpu.PrefetchScalarGridSpec(
            num_scalar_prefetch=3,
            grid=(n_steps,),
            in_specs=[hbm, hbm, hbm, hbm],
            out_specs=hbm,
            scratch_shapes=[
                pltpu.VMEM((tm, D_MODEL), F32),
                pltpu.VMEM((tm, D_MODEL), BF16),
                pltpu.VMEM((tm, d_ff), BF16),
                pltpu.VMEM((tm, D_MODEL), F32),
                pltpu.VMEM((D_MODEL, d_ff), BF16),
                pltpu.VMEM((D_MODEL, d_ff), BF16),
                pltpu.VMEM((d_ff, D_MODEL), BF16),
                pltpu.VMEM((2, D_MODEL, FF_CHUNK), F32),
                pltpu.VMEM((2, FF_CHUNK, D_MODEL), F32),
                pltpu.SemaphoreType.DMA,
                pltpu.SemaphoreType.DMA,
                pltpu.SemaphoreType.DMA((2,)),
            ],
        ),
        out_shape=jax.ShapeDtypeStruct((n_y_rows, D_MODEL), F32),
        compiler_params=pltpu.CompilerParams(
            dimension_semantics=("arbitrary",), vmem_limit_bytes=VMEM_LIMIT_BYTES),
        name="ffn_moe",
    )(tile_expert, n_used, code, h2d, wg_all, wu_all, wd_all)


COMBINE_TS = 1024


def _combine_kernel(h_ref, wts_ref, y1_ref, y2_ref, lg_ref, lb_ref, o_ref):
    w = wts_ref[...]
    f = w[:, 0:1] * y1_ref[...] + w[:, 1:2] * y2_ref[...]
    o_ref[...] = _layer_norm_rows(ALPHA * h_ref[...] + f, lg_ref[...], lb_ref[...])


def _combine(h2d, wts, y, ln_g, ln_b):
    n_tok = h2d.shape[0]
    ts = COMBINE_TS
    assert n_tok % ts == 0
    row = lambda a: a.reshape(1, -1)
    second = n_tok // ts
    return pl.pallas_call(
        _combine_kernel,
        grid=(n_tok // ts,),
        in_specs=[
            pl.BlockSpec((ts, D_MODEL), lambda i: (i, 0)),
            pl.BlockSpec((ts, ROUTE_COLS), lambda i: (i, 0)),
            pl.BlockSpec((ts, D_MODEL), lambda i: (i, 0)),
            pl.BlockSpec((ts, D_MODEL), lambda i: (second + i, 0)),
            pl.BlockSpec((1, D_MODEL), lambda i: (0, 0)),
            pl.BlockSpec((1, D_MODEL), lambda i: (0, 0)),
        ],
        out_specs=pl.BlockSpec((ts, D_MODEL), lambda i: (i, 0)),
        out_shape=jax.ShapeDtypeStruct((n_tok, D_MODEL), F32),
        compiler_params=pltpu.CompilerParams(
            dimension_semantics=("arbitrary",), vmem_limit_bytes=VMEM_LIMIT_BYTES),
        name="moe_combine",
    )(h2d, wts, y, y, row(ln_g), row(ln_b))


def _moe_layer(h2d, w_router, b_router, layer, wg_all, wu_all, wd_all, ln_g, ln_b):
    n_tok = h2d.shape[0]
    tm = MOE_TM
    idx, wts, cnt = _route(h2d, w_router, b_router)

    counts = cnt[:, 0].astype(jnp.int32)
    padded = ((counts + tm - 1) // tm) * tm
    ends = jnp.cumsum(padded)
    starts = ends - padded
    n_steps = (TOP_K * n_tok) // tm + N_EXPERTS + 1
    n_steps += n_steps % 2
    n_used = (ends[-1] // tm).reshape(1)
    tile_id = jnp.minimum(jnp.arange(n_steps, dtype=jnp.int32), n_used - 1)
    tile_expert = jnp.minimum(
        jnp.sum((ends[None, :] <= (tile_id * tm)[:, None]).astype(jnp.int32), axis=1), N_EXPERTS - 1)
    start_of = lambda e: sum(jnp.where(e == k, starts[k], 0) for k in range(N_EXPERTS))
    pos1 = start_of(idx[0]) + idx[2]
    pos2 = start_of(idx[1]) + idx[3]

    n_code = n_steps * tm
    code = _invert(pos1, pos2, n_code)
    y = _ffn_moe(h2d, tile_expert, n_used, code, TOP_K * n_tok + n_code, layer, wg_all, wu_all, wd_all)
    return _combine(h2d, wts, y, ln_g, ln_b)


def kernel(x, w_in, b_gates, conv_w, conv_b, pool_w, pool_scale, mlstm_g, w_out, ln1_g, ln1_b, ln2_g, ln2_b, wg_dense, wu_dense, wd_dense, w_router, b_router, wg_exp, wu_exp, wd_exp):
    bsz, seq, d = x.shape
    x2d = x.reshape(bsz * seq, d)
    for l in range(DEPTH):
        h2d = _mixer_layer(x2d, bsz, seq, w_in[l], b_gates[l], conv_w[l], conv_b[l], pool_w[l],
                           pool_scale[l], mlstm_g[l], w_out[l], ln1_g[l], ln1_b[l])
        j = l // 2
        if l % 2 == 0:
            x2d = _ffn_dense_layer(h2d, wg_dense[j], wu_dense[j], wd_dense[j], ln2_g[l], ln2_b[l])
        else:
            x2d = _moe_layer(h2d, w_router[j], b_router[j], j, wg_exp, wu_exp, wd_exp,
                             ln2_g[l], ln2_b[l])
    return x2d.reshape(bsz, seq, d)
```
